```python
import math
import jax, jax.numpy as jnp
from jax import lax
import numpy as np

D_MODEL = 2048
BATCH = 8
SEQ = 2048
DEPTH = 2

BRANCH = D_MODEL
HEAD_DIM = 128
FOX_HEADS = BRANCH // HEAD_DIM
DIFF_HEADS = BRANCH // (2 * HEAD_DIM)
DIFF_VDIM = 2 * HEAD_DIM
N_MIXERS = 2
N_FOX = (DEPTH + 1) // 2
N_DIFF = DEPTH // 2
FOX_IN = 4 * BRANCH + FOX_HEADS
DIFF_IN = 4 * BRANCH
NUM_BUCKETS = 32
MAX_DISTANCE = 128
Q_BLOCK = 128
ALPHA = (2 * DEPTH) ** 0.25
BETA = (8 * DEPTH) ** -0.25
LN_EPS = 1e-5
RMS_EPS = 1e-5

kernel_name = 'hybrid_fox_diffattn_deepnorm'


def layer_norm(x, g, b):
    xf = x.astype(jnp.float32)
    mu = jnp.mean(xf, axis=-1, keepdims=True)
    xc = xf - mu
    var = jnp.mean(xc * xc, axis=-1, keepdims=True)
    y = xc * lax.rsqrt(var + LN_EPS) * g.astype(jnp.float32) + b.astype(jnp.float32)
    return y.astype(x.dtype)


def causal_mask(q0, kl):
    qpos = q0 + jnp.arange(Q_BLOCK, dtype=jnp.int32)[:, None]
    kpos = jnp.arange(kl, dtype=jnp.int32)[None, :]
    return kpos <= qpos, qpos - kpos


def t5_bucket(rel):
    n = jnp.maximum(rel, 0)
    max_exact = NUM_BUCKETS // 2
    large = max_exact + (jnp.log(jnp.maximum(n, 1).astype(jnp.float32) / max_exact)
                         / math.log(MAX_DISTANCE / max_exact)
                         * (NUM_BUCKETS - max_exact)).astype(jnp.int32)
    large = jnp.minimum(large, NUM_BUCKETS - 1)
    return jnp.where(n < max_exact, n, large)


def fox_mixer(h, w_in, b_f, w_out):
    B, S, _ = h.shape
    proj = h @ w_in
    q, k, v, z, f = jnp.split(proj, [BRANCH, 2 * BRANCH, 3 * BRANCH, 4 * BRANCH], axis=-1)
    q = q.reshape(B, S, FOX_HEADS, HEAD_DIM)
    k = k.reshape(B, S, FOX_HEADS, HEAD_DIM)
    v = v.reshape(B, S, FOX_HEADS, HEAD_DIM)
    logf = jax.nn.log_sigmoid((f + b_f).astype(jnp.float32))
    c = jnp.cumsum(logf, axis=1).transpose(0, 2, 1)
    scale = HEAD_DIM ** -0.5
    outs = []
    for i in range(S // Q_BLOCK):
        q0 = i * Q_BLOCK
        kl = q0 + Q_BLOCK
        s = jnp.einsum('bqhd,bkhd->bhqk', q[:, q0:kl], k[:, :kl],
                       preferred_element_type=jnp.float32) * scale
        s = s + c[:, :, q0:kl, None] - c[:, :, None, :kl]
        mask, _ = causal_mask(q0, kl)
        s = jnp.where(mask, s, -jnp.inf)
        p = jax.nn.softmax(s, axis=-1)
        outs.append(jnp.einsum('bhqk,bkhd->bqhd', p.astype(v.dtype), v[:, :kl]))
    o = jnp.concatenate(outs, axis=1).reshape(B, S, BRANCH)
    return (o * jax.nn.silu(z)) @ w_out


def diff_mixer(h, w_in, lam_q, lam_k, subln_g, w_out, rel_bias, layer_idx):
    B, S, _ = h.shape
    proj = h @ w_in
    q, k, v, z = jnp.split(proj, [BRANCH, 2 * BRANCH, 3 * BRANCH], axis=-1)
    q = q.reshape(B, S, DIFF_HEADS, 2, HEAD_DIM)
    k = k.reshape(B, S, DIFF_HEADS, 2, HEAD_DIM)
    v = v.reshape(B, S, DIFF_HEADS, DIFF_VDIM)
    lambda_init = 0.8 - 0.6 * math.exp(-0.3 * layer_idx)
    lq = lam_q.astype(jnp.float32)
    lk = lam_k.astype(jnp.float32)
    lam = jnp.exp(jnp.sum(lq[0] * lk[0])) - jnp.exp(jnp.sum(lq[1] * lk[1])) + lambda_init
    scale = HEAD_DIM ** -0.5
    table = rel_bias.astype(jnp.float32)
    outs = []
    for i in range(S // Q_BLOCK):
        q0 = i * Q_BLOCK
        kl = q0 + Q_BLOCK
        mask, rel = causal_mask(q0, kl)
        bias = table[t5_bucket(rel)].transpose(2, 0, 1)
        s = jnp.einsum('bqhmd,bkhmd->bhmqk', q[:, q0:kl], k[:, :kl],
                       preferred_element_type=jnp.float32) * scale
        s = s + bias[None, :, None]
        s = jnp.where(mask, s, -jnp.inf)
        p = jax.nn.softmax(s, axis=-1)
        a = p[:, :, 0] - lam * p[:, :, 1]
        outs.append(jnp.einsum('bhqk,bkhe->bqhe', a.astype(v.dtype), v[:, :kl]))
    o = jnp.concatenate(outs, axis=1).astype(jnp.float32)
    o = o * lax.rsqrt(jnp.mean(o * o, axis=-1, keepdims=True) + RMS_EPS) * subln_g.astype(jnp.float32)
    o = (o * (1.0 - lambda_init)).astype(z.dtype).reshape(B, S, BRANCH)
    return (o * jax.nn.silu(z)) @ w_out


def setup_inputs(seed: int = 0) -> dict:
    key = jax.random.key(seed)
    ks = jax.random.split(key, 13)
    f32 = jnp.float32
    x = jax.random.normal(ks[0], (BATCH, SEQ, D_MODEL), f32)
    fox_w_in = jax.random.normal(ks[1], (N_FOX, D_MODEL, FOX_IN), f32) * D_MODEL ** -0.5
    fox_b_f = jax.random.uniform(ks[2], (N_FOX, FOX_HEADS), f32, minval=1.0, maxval=6.0)
    fox_w_out = jax.random.normal(ks[3], (N_FOX, BRANCH, D_MODEL), f32) * (BRANCH ** -0.5) * BETA
    diff_w_in = jax.random.normal(ks[4], (N_DIFF, D_MODEL, DIFF_IN), f32) * D_MODEL ** -0.5
    diff_lam_q = jax.random.normal(ks[5], (N_DIFF, 2, HEAD_DIM), f32) * 0.1
    diff_lam_k = jax.random.normal(ks[6], (N_DIFF, 2, HEAD_DIM), f32) * 0.1
    diff_subln_g = 1.0 + 0.02 * jax.random.normal(ks[7], (N_DIFF, DIFF_VDIM), f32)
    diff_w_out = jax.random.normal(ks[8], (N_DIFF, BRANCH, D_MODEL), f32) * (BRANCH ** -0.5) * BETA
    rel_bias = jax.random.normal(ks[9], (NUM_BUCKETS, DIFF_HEADS), f32) * 0.5
    ln_g = 1.0 + 0.02 * jax.random.normal(ks[10], (DEPTH, D_MODEL), f32)
    ln_b = 0.02 * jax.random.normal(ks[11], (DEPTH, D_MODEL), f32)
    return {'x': x, 'fox_w_in': fox_w_in, 'fox_b_f': fox_b_f, 'fox_w_out': fox_w_out,
            'diff_w_in': diff_w_in, 'diff_lam_q': diff_lam_q, 'diff_lam_k': diff_lam_k,
            'diff_subln_g': diff_subln_g, 'diff_w_out': diff_w_out, 'rel_bias': rel_bias,
            'ln_g': ln_g, 'ln_b': ln_b}


def reference(x, fox_w_in, fox_b_f, fox_w_out, diff_w_in, diff_lam_q, diff_lam_k,
              diff_subln_g, diff_w_out, rel_bias, ln_g, ln_b):
    for i in range(DEPTH):
        j = i // N_MIXERS
        if i % N_MIXERS == 0:
            y = fox_mixer(x, fox_w_in[j], fox_b_f[j], fox_w_out[j])
        else:
            y = diff_mixer(x, diff_w_in[j], diff_lam_q[j], diff_lam_k[j], diff_subln_g[j],
                           diff_w_out[j], rel_bias, i)
        x = layer_norm(ALPHA * x + y, ln_g[i], ln_b[i])
    return x
```

```python
import functools
import math

import jax
import jax.numpy as jnp
from jax import lax
from jax.experimental import pallas as pl
from jax.experimental.pallas import tpu as pltpu

HEAD_DIM = 128
FOX_HEADS = 16
DIFF_HEADS = 8
NUM_BUCKETS = 32
MAX_DISTANCE = 128
LN_EPS = 1e-5
RMS_EPS = 1e-5
MASK_VALUE = -1e30

V7X_VMEM_LIMIT = 56 * 1024 * 1024

PROJ_TM = 1024
PROJ_TN = 1024
FORGET_TS = 512
ATTN_T = 256
OUT_TM = 256

_NT = (((1,), (1,)), ((), ()))


def _params(*sem):
    return pltpu.CompilerParams(dimension_semantics=sem, vmem_limit_bytes=V7X_VMEM_LIMIT)


def _proj_kernel(x_ref, w_ref, o_ref, xb_ref, *, n_scaled_tiles, scale):
    j = pl.program_id(1)

    @pl.when(j == 0)
    def _():
        xb_ref[...] = x_ref[...].astype(jnp.bfloat16)

    acc = jnp.dot(xb_ref[...], w_ref[...], preferred_element_type=jnp.float32)
    s = jnp.where(j < n_scaled_tiles, jnp.float32(scale), jnp.float32(1.0))
    o_ref[...] = (acc * s).astype(o_ref.dtype)


def _project(x2d, w_bf16, q_cols, scale):
    m, k = x2d.shape
    n = w_bf16.shape[1]
    tm, tn = PROJ_TM, PROJ_TN
    kern = functools.partial(_proj_kernel, n_scaled_tiles=q_cols // tn, scale=scale)
    return pl.pallas_call(
        kern,
        grid=(m // tm, n // tn),
        in_specs=[pl.BlockSpec((tm, k), lambda i, j: (i, 0)),
                  pl.BlockSpec((k, tn), lambda i, j: (0, j))],
        out_specs=pl.BlockSpec((tm, tn), lambda i, j: (i, j)),
        out_shape=jax.ShapeDtypeStruct((m, n), jnp.bfloat16),
        scratch_shapes=[pltpu.VMEM((tm, k), jnp.bfloat16)],
        compiler_params=_params("parallel", "arbitrary"),
        name="in_proj",
    )(x2d, w_bf16)


def _split3(a):
    p1 = a.astype(jnp.bfloat16)
    r1 = a - p1.astype(jnp.float32)
    p2 = r1.astype(jnp.bfloat16)
    p3 = (r1 - p2.astype(jnp.float32)).astype(jnp.bfloat16)
    return p1, p2, p3


def _forget_kernel(x_ref, wt_ref, b_ref, c_ref, carry_ref):
    t = pl.program_id(1)
    ts = x_ref.shape[0]

    @pl.when(t == 0)
    def _():
        carry_ref[...] = jnp.zeros_like(carry_ref)

    x = x_ref[...]
    w = wt_ref[...]
    xh = x.astype(jnp.bfloat16)
    xl = (x - xh.astype(jnp.float32)).astype(jnp.bfloat16)
    wh = w.astype(jnp.bfloat16)
    wl = (w - wh.astype(jnp.float32)).astype(jnp.bfloat16)
    dot = functools.partial(lax.dot_general, dimension_numbers=_NT,
                            preferred_element_type=jnp.float32)
    f = dot(wh, xh) + (dot(wl, xh) + dot(wh, xl))
    z = f + b_ref[...]
    logf = jnp.minimum(z, 0.0) - jnp.log1p(jnp.exp(-jnp.abs(z)))
    row = lax.broadcasted_iota(jnp.int32, (ts, ts), 0)
    col = lax.broadcasted_iota(jnp.int32, (ts, ts), 1)
    tri = (row <= col).astype(jnp.bfloat16)
    p1, p2, p3 = _split3(logf)
    csum = (jnp.dot(p1, tri, preferred_element_type=jnp.float32)
            + jnp.dot(p2, tri, preferred_element_type=jnp.float32)
            + jnp.dot(p3, tri, preferred_element_type=jnp.float32))
    c = csum + carry_ref[...]
    c_ref[...] = c
    carry_ref[...] = c[:, ts - 1:ts]


def _forget_cumsum(x, wf_t, b_f):
    bsz, s, d = x.shape
    h = wf_t.shape[0]
    ts = FORGET_TS
    return pl.pallas_call(
        _forget_kernel,
        grid=(bsz, s // ts),
        in_specs=[pl.BlockSpec((None, ts, d), lambda b, t: (b, t, 0)),
                  pl.BlockSpec((h, d), lambda b, t: (0, 0)),
                  pl.BlockSpec((h, 1), lambda b, t: (0, 0))],
        out_specs=pl.BlockSpec((None, h, ts), lambda b, t: (b, 0, t)),
        out_shape=jax.ShapeDtypeStruct((bsz, h, s), jnp.float32),
        scratch_shapes=[pltpu.VMEM((h, 1), jnp.float32)],
        compiler_params=_params("parallel", "arbitrary"),
        name="forget_cumsum",
    )(x, wf_t, b_f.reshape(h, 1))


def _softmax_step(s, v, m_ref, l_ref, acc_ref):
    m_prev = m_ref[...]
    m_new = jnp.maximum(m_prev, jnp.max(s, axis=-1, keepdims=True))
    alpha = jnp.exp(m_prev - m_new)
    p = jnp.exp(s - m_new)
    l_ref[...] = alpha * l_ref[...] + jnp.sum(p, axis=-1, keepdims=True)
    acc_ref[...] = alpha * acc_ref[...] + jnp.dot(p.astype(v.dtype), v,
                                                  preferred_element_type=jnp.float32)
    m_ref[...] = m_new


def _causal_tile_mask(t):
    row = lax.broadcasted_iota(jnp.int32, (t, t), 0)
    col = lax.broadcasted_iota(jnp.int32, (t, t), 1)
    return col <= row


def _fox_attn_kernel(q_ref, k_ref, v_ref, c_ref, o_ref, m_ref, l_ref, acc_ref):
    i = pl.program_id(2)
    t = q_ref.shape[0]
    q = q_ref[...]
    m_ref[...] = jnp.full_like(m_ref, MASK_VALUE)
    l_ref[...] = jnp.zeros_like(l_ref)
    acc_ref[...] = jnp.zeros_like(acc_ref)

    def scores(j):
        start = pl.multiple_of(j * t, t)
        k = k_ref[pl.ds(start, t), :]
        s = lax.dot_general(q, k, _NT, preferred_element_type=jnp.float32)
        return s - c_ref[:, pl.ds(start, t)], v_ref[pl.ds(start, t), :]

    def body(j, carry):
        s, v = scores(j)
        _softmax_step(s, v, m_ref, l_ref, acc_ref)
        return carry

    lax.fori_loop(0, i, body, 0)
    s, v = scores(i)
    s = jnp.where(_causal_tile_mask(t), s, MASK_VALUE)
    _softmax_step(s, v, m_ref, l_ref, acc_ref)
    o_ref[...] = (acc_ref[...] / l_ref[...]).astype(o_ref.dtype)


def _fox_attention(proj, c, bsz, s):
    t = ATTN_T
    h, d = FOX_HEADS, HEAD_DIM
    return pl.pallas_call(
        _fox_attn_kernel,
        grid=(bsz, h, s // t),
        in_specs=[pl.BlockSpec((None, t, d), lambda b, hh, i: (b, i, hh)),
                  pl.BlockSpec((None, s, d), lambda b, hh, i: (b, 0, h + hh)),
                  pl.BlockSpec((None, s, d), lambda b, hh, i: (b, 0, 2 * h + hh)),
                  pl.BlockSpec((None, None, 1, s), lambda b, hh, i: (b, hh, 0, 0))],
        out_specs=pl.BlockSpec((None, t, d), lambda b, hh, i: (b, i, hh)),
        out_shape=jax.ShapeDtypeStruct((bsz, s, h * d), jnp.bfloat16),
        scratch_shapes=[pltpu.VMEM((t, 1), jnp.float32),
                        pltpu.VMEM((t, 1), jnp.float32),
                        pltpu.VMEM((t, d), jnp.float32)],
        compiler_params=_params("parallel", "parallel", "arbitrary"),
        name="fox_attn",
    )(proj, proj, proj, c)


def _bias_kernel(tab_ref, o_ref):
    h = pl.program_id(0)
    r = pl.program_id(1)
    t = o_ref.shape[0]
    dq = lax.broadcasted_iota(jnp.int32, (t, t), 0)
    dk = lax.broadcasted_iota(jnp.int32, (t, t), 1)
    rel = r * t + dq - dk
    n = jnp.maximum(rel, 0)
    max_exact = NUM_BUCKETS // 2
    large = max_exact + (jnp.log(jnp.maximum(n, 1).astype(jnp.float32) / max_exact)
                         / math.log(MAX_DISTANCE / max_exact)
                         * (NUM_BUCKETS - max_exact)).astype(jnp.int32)
    large = jnp.minimum(large, NUM_BUCKETS - 1)
    bucket = jnp.where(n < max_exact, n, large)
    val = jnp.zeros((t, t), jnp.float32)
    for b in range(NUM_BUCKETS):
        val = jnp.where(bucket == b, tab_ref[b, h], val)
    val = val - tab_ref[NUM_BUCKETS - 1, h]
    o_ref[...] = jnp.where(rel >= 0, val, MASK_VALUE)


def _bias_tiles(rel_bias, t):
    nh = rel_bias.shape[1]
    return pl.pallas_call(
        _bias_kernel,
        grid=(nh, 2),
        in_specs=[pl.BlockSpec(memory_space=pltpu.SMEM)],
        out_specs=pl.BlockSpec((None, None, t, t), lambda h, r: (h, r, 0, 0)),
        out_shape=jax.ShapeDtypeStruct((nh, 2, t, t), jnp.float32),
        compiler_params=_params("parallel", "parallel"),
        name="bias_tiles",
    )(rel_bias)


def _diff_attn_kernel(q_ref, k_ref, v_ref, bias_ref, lq_ref, lk_ref, g_ref, o_ref,
                      m1_ref, l1_ref, acc1_ref, m2_ref, l2_ref, acc2_ref, *, lambda_init):
    i = pl.program_id(2)
    t = q_ref.shape[0]
    d = HEAD_DIM
    q1 = q_ref[:, :d]
    q2 = q_ref[:, d:]
    stats = ((q1, m1_ref, l1_ref, acc1_ref), (q2, m2_ref, l2_ref, acc2_ref))
    for _, m_ref, l_ref, acc_ref in stats:
        m_ref[...] = jnp.full_like(m_ref, MASK_VALUE)
        l_ref[...] = jnp.zeros_like(l_ref)
        acc_ref[...] = jnp.zeros_like(acc_ref)

    def step(j, bias):
        start = pl.multiple_of(j * t, t)
        k = k_ref[pl.ds(start, t), :]
        v = v_ref[pl.ds(start, t), :]
        for idx, (qm, m_ref, l_ref, acc_ref) in enumerate(stats):
            s = lax.dot_general(qm, k[:, idx * d:(idx + 1) * d], _NT,
                                preferred_element_type=jnp.float32)
            if bias is not None:
                s = s + bias
            _softmax_step(s, v, m_ref, l_ref, acc_ref)

    def far_body(j, carry):
        step(j, None)
        return carry

    lax.fori_loop(0, i - 1, far_body, 0)

    @pl.when(i > 0)
    def _():
        step(i - 1, bias_ref[1])

    step(i, bias_ref[0])

    lq = lq_ref[...]
    lk = lk_ref[...]
    dots = jnp.sum(lq * lk, axis=-1, keepdims=True)
    lam = jnp.exp(dots[0:1]) - jnp.exp(dots[1:2]) + lambda_init
    o = acc1_ref[...] / l1_ref[...] - lam * (acc2_ref[...] / l2_ref[...])
    o = o * lax.rsqrt(jnp.mean(o * o, axis=-1, keepdims=True) + RMS_EPS) * g_ref[...]
    o_ref[...] = (o * (1.0 - lambda_init)).astype(o_ref.dtype)


def _diff_attention(proj, bias, lam_q, lam_k, subln_g, bsz, s, lambda_init):
    t = ATTN_T
    h, d2 = DIFF_HEADS, 2 * HEAD_DIM
    kern = functools.partial(_diff_attn_kernel, lambda_init=lambda_init)
    return pl.pallas_call(
        kern,
        grid=(bsz, h, s // t),
        in_specs=[pl.BlockSpec((None, t, d2), lambda b, hh, i: (b, i, hh)),
                  pl.BlockSpec((None, s, d2), lambda b, hh, i: (b, 0, h + hh)),
                  pl.BlockSpec((None, s, d2), lambda b, hh, i: (b, 0, 2 * h + hh)),
                  pl.BlockSpec((None, 2, t, t), lambda b, hh, i: (hh, 0, 0, 0)),
                  pl.BlockSpec((2, HEAD_DIM), lambda b, hh, i: (0, 0)),
                  pl.BlockSpec((2, HEAD_DIM), lambda b, hh, i: (0, 0)),
                  pl.BlockSpec((1, d2), lambda b, hh, i: (0, 0))],
        out_specs=pl.BlockSpec((None, t, d2), lambda b, hh, i: (b, i, hh)),
        out_shape=jax.ShapeDtypeStruct((bsz, s, h * d2), jnp.bfloat16),
        scratch_shapes=[pltpu.VMEM((t, 1), jnp.float32), pltpu.VMEM((t, 1), jnp.float32),
                        pltpu.VMEM((t, d2), jnp.float32),
                        pltpu.VMEM((t, 1), jnp.float32), pltpu.VMEM((t, 1), jnp.float32),
                        pltpu.VMEM((t, d2), jnp.float32)],
        compiler_params=_params("parallel", "parallel", "arbitrary"),
        name="diff_attn",
    )(proj, proj, proj, bias, lam_q, lam_k, subln_g.reshape(1, d2))


def _out_ln_kernel(o_ref, z_ref, x_ref, w_ref, g_ref, b_ref, y_ref, *, alpha):
    o = o_ref[...].astype(jnp.float32)
    z = z_ref[...].astype(jnp.float32)
    gated = o * (z / (1.0 + jnp.exp(-z)))
    y = jnp.dot(gated.astype(jnp.bfloat16), w_ref[...], preferred_element_type=jnp.float32)
    v = alpha * x_ref[...] + y
    mu = jnp.mean(v, axis=-1, keepdims=True)
    vc = v - mu
    var = jnp.mean(vc * vc, axis=-1, keepdims=True)
    y_ref[...] = vc * lax.rsqrt(var + LN_EPS) * g_ref[...] + b_ref[...]


def _out_ln(o2d, proj2d, x2d, w_bf16, g, b, alpha):
    m, d = x2d.shape
    br = o2d.shape[1]
    tm = OUT_TM
    z_block = proj2d.shape[1] // br - 1
    kern = functools.partial(_out_ln_kernel, alpha=alpha)
    return pl.pallas_call(
        kern,
        grid=(m // tm,),
        in_specs=[pl.BlockSpec((tm, br), lambda i: (i, 0)),
                  pl.BlockSpec((tm, br), lambda i: (i, z_block)),
                  pl.BlockSpec((tm, d), lambda i: (i, 0)),
                  pl.BlockSpec((br, d), lambda i: (0, 0)),
                  pl.BlockSpec((1, d), lambda i: (0, 0)),
                  pl.BlockSpec((1, d), lambda i: (0, 0))],
        out_specs=pl.BlockSpec((tm, d), lambda i: (i, 0)),
        out_shape=jax.ShapeDtypeStruct((m, d), jnp.float32),
        compiler_params=_params("parallel"),
        name="out_ln",
    )(o2d, proj2d, x2d, w_bf16, g.reshape(1, d), b.reshape(1, d))


def kernel(x, fox_w_in, fox_b_f, fox_w_out, diff_w_in, diff_lam_q, diff_lam_k,
           diff_subln_g, diff_w_out, rel_bias, ln_g, ln_b):
    bsz, s, d = x.shape
    depth = ln_g.shape[0]
    branch = fox_w_out.shape[1]
    alpha = (2 * depth) ** 0.25
    scale = HEAD_DIM ** -0.5
    m = bsz * s
    bf16 = jnp.bfloat16

    x2d = x.reshape(m, d)
    for i in range(depth):
        j = i // 2
        if i % 2 == 0:
            w_in = fox_w_in[j]
            proj = _project(x2d, w_in[:, :4 * branch].astype(bf16), branch, scale)
            c = _forget_cumsum(x2d.reshape(bsz, s, d), w_in[:, 4 * branch:].T, fox_b_f[j])
            o = _fox_attention(proj.reshape(bsz, s, 4 * branch),
                               c.reshape(bsz, FOX_HEADS, 1, s), bsz, s)
            w_out = fox_w_out[j]
        else:
            lambda_init = 0.8 - 0.6 * math.exp(-0.3 * i)
            proj = _project(x2d, diff_w_in[j].astype(bf16), branch, scale)
            bias = _bias_tiles(rel_bias, ATTN_T)
            o = _diff_attention(proj.reshape(bsz, s, 4 * branch), bias, diff_lam_q[j],
                                diff_lam_k[j], diff_subln_g[j], bsz, s, lambda_init)
            w_out = diff_w_out[j]
        x2d = _out_ln(o.reshape(m, branch), proj, x2d, w_out.astype(bf16),
                      ln_g[i], ln_b[i], alpha)
    return x2d.reshape(bsz, s, d)
```

```python
import functools
import math

import jax
import jax.numpy as jnp
from jax import lax
from jax.experimental import pallas as pl
from jax.experimental.pallas import tpu as pltpu

HEAD_DIM = 128
FOX_HEADS = 16
DIFF_HEADS = 8
NUM_BUCKETS = 32
MAX_DISTANCE = 128
LN_EPS = 1e-5
RMS_EPS = 1e-5
MASK_VALUE = -1e30

V7X_VMEM_LIMIT = 56 * 1024 * 1024

PROJ_TM = 1024
PROJ_TN = 1024
FORGET_TS = 512
ATTN_TQ = 256
OUT_TM = 256

_NT = (((1,), (1,)), ((), ()))


def _params(*sem):
    return pltpu.CompilerParams(dimension_semantics=sem, vmem_limit_bytes=V7X_VMEM_LIMIT)


def _proj_kernel(x_ref, w_ref, o_ref, xb_ref, *, n_scaled_tiles, scale):
    j = pl.program_id(1)

    @pl.when(j == 0)
    def _():
        xb_ref[...] = x_ref[...].astype(jnp.bfloat16)

    acc = jnp.dot(xb_ref[...], w_ref[...], preferred_element_type=jnp.float32)
    s = jnp.where(j < n_scaled_tiles, jnp.float32(scale), jnp.float32(1.0))
    o_ref[...] = (acc * s).astype(o_ref.dtype)


def _project(x2d, w_bf16, q_cols, scale):
    m, k = x2d.shape
    n = w_bf16.shape[1]
    tm, tn = PROJ_TM, PROJ_TN
    kern = functools.partial(_proj_kernel, n_scaled_tiles=q_cols // tn, scale=scale)
    return pl.pallas_call(
        kern,
        grid=(m // tm, n // tn),
        in_specs=[pl.BlockSpec((tm, k), lambda i, j: (i, 0)),
                  pl.BlockSpec((k, tn), lambda i, j: (0, j))],
        out_specs=pl.BlockSpec((tm, tn), lambda i, j: (i, j)),
        out_shape=jax.ShapeDtypeStruct((m, n), jnp.bfloat16),
        scratch_shapes=[pltpu.VMEM((tm, k), jnp.bfloat16)],
        compiler_params=_params("parallel", "arbitrary"),
        name="in_proj",
    )(x2d, w_bf16)


def _split3(a):
    p1 = a.astype(jnp.bfloat16)
    r1 = a - p1.astype(jnp.float32)
    p2 = r1.astype(jnp.bfloat16)
    p3 = (r1 - p2.astype(jnp.float32)).astype(jnp.bfloat16)
    return p1, p2, p3


def _forget_kernel(x_ref, wt_ref, b_ref, c_ref, carry_ref):
    t = pl.program_id(1)
    ts = x_ref.shape[0]

    @pl.when(t == 0)
    def _():
        carry_ref[...] = jnp.zeros_like(carry_ref)

    x = x_ref[...]
    w = wt_ref[...]
    xh = x.astype(jnp.bfloat16)
    xl = (x - xh.astype(jnp.float32)).astype(jnp.bfloat16)
    wh = w.astype(jnp.bfloat16)
    wl = (w - wh.astype(jnp.float32)).astype(jnp.bfloat16)
    dot = functools.partial(lax.dot_general, dimension_numbers=_NT,
                            preferred_element_type=jnp.float32)
    f = dot(wh, xh) + (dot(wl, xh) + dot(wh, xl))
    z = f + b_ref[...]
    logf = jnp.minimum(z, 0.0) - jnp.log1p(jnp.exp(-jnp.abs(z)))
    row = lax.broadcasted_iota(jnp.int32, (ts, ts), 0)
    col = lax.broadcasted_iota(jnp.int32, (ts, ts), 1)
    tri = (row <= col).astype(jnp.bfloat16)
    p1, p2, p3 = _split3(logf)
    csum = (jnp.dot(p1, tri, preferred_element_type=jnp.float32)
            + jnp.dot(p2, tri, preferred_element_type=jnp.float32)
            + jnp.dot(p3, tri, preferred_element_type=jnp.float32))
    c = csum + carry_ref[...]
    c_ref[...] = c
    carry_ref[...] = c[:, ts - 1:ts]


def _forget_cumsum(x, wf_t, b_f):
    bsz, s, d = x.shape
    h = wf_t.shape[0]
    ts = FORGET_TS
    return pl.pallas_call(
        _forget_kernel,
        grid=(bsz, s // ts),
        in_specs=[pl.BlockSpec((None, ts, d), lambda b, t: (b, t, 0)),
                  pl.BlockSpec((h, d), lambda b, t: (0, 0)),
                  pl.BlockSpec((h, 1), lambda b, t: (0, 0))],
        out_specs=pl.BlockSpec((None, h, ts), lambda b, t: (b, 0, t)),
        out_shape=jax.ShapeDtypeStruct((bsz, h, s), jnp.float32),
        scratch_shapes=[pltpu.VMEM((h, 1), jnp.float32)],
        compiler_params=_params("parallel", "arbitrary"),
        name="forget_cumsum",
    )(x, wf_t, b_f.reshape(h, 1))


def _causal_tile_mask(t):
    row = lax.broadcasted_iota(jnp.int32, (t, t), 0)
    col = lax.broadcasted_iota(jnp.int32, (t, t), 1)
    return col <= row


def _nt_dot(a, b):
    return lax.dot_general(a, b, _NT, preferred_element_type=jnp.float32)


def _fox_attn_kernel(q_ref, k_ref, v_ref, c_ref, o_ref, *, tq):
    n = q_ref.shape[0] // tq
    mask = _causal_tile_mask(tq)

    def scores(i):
        lo = i * tq
        q = q_ref[lo:lo + tq, :]
        parts = []
        if i > 0:
            parts.append((_nt_dot(q, k_ref[:lo, :]) - c_ref[:, :lo], 0, lo))
        sd = _nt_dot(q, k_ref[lo:lo + tq, :]) - c_ref[:, lo:lo + tq]
        parts.append((jnp.where(mask, sd, MASK_VALUE), lo, lo + tq))
        return parts

    def finish(i, parts):
        m = functools.reduce(jnp.maximum,
                             [jnp.max(s, axis=-1, keepdims=True) for s, _, _ in parts])
        l = None
        acc = None
        for s, lo, hi in parts:
            p = jnp.exp(s - m)
            ps = jnp.sum(p, axis=-1, keepdims=True)
            pv = jnp.dot(p.astype(v_ref.dtype), v_ref[lo:hi, :],
                         preferred_element_type=jnp.float32)
            l = ps if l is None else l + ps
            acc = pv if acc is None else acc + pv
        o_ref[i * tq:(i + 1) * tq, :] = (acc / l).astype(o_ref.dtype)

    nxt = scores(0)
    for i in range(n):
        cur = nxt
        if i + 1 < n:
            nxt = scores(i + 1)
        finish(i, cur)


def _fox_attention(proj, c, bsz, s):
    h, d = FOX_HEADS, HEAD_DIM
    kern = functools.partial(_fox_attn_kernel, tq=ATTN_TQ)
    return pl.pallas_call(
        kern,
        grid=(bsz, h),
        in_specs=[pl.BlockSpec((None, s, d), lambda b, hh: (b, 0, hh)),
                  pl.BlockSpec((None, s, d), lambda b, hh: (b, 0, h + hh)),
                  pl.BlockSpec((None, s, d), lambda b, hh: (b, 0, 2 * h + hh)),
                  pl.BlockSpec((None, None, 1, s), lambda b, hh: (b, hh, 0, 0))],
        out_specs=pl.BlockSpec((None, s, d), lambda b, hh: (b, 0, hh)),
        out_shape=jax.ShapeDtypeStruct((bsz, s, h * d), jnp.bfloat16),
        compiler_params=_params("parallel", "parallel"),
        name="fox_attn",
    )(proj, proj, proj, c)


def _bias_kernel(tab_ref, o_ref):
    h = pl.program_id(0)
    r = pl.program_id(1)
    t = o_ref.shape[0]
    dq = lax.broadcasted_iota(jnp.int32, (t, t), 0)
    dk = lax.broadcasted_iota(jnp.int32, (t, t), 1)
    rel = r * t + dq - dk
    n = jnp.maximum(rel, 0)
    max_exact = NUM_BUCKETS // 2
    large = max_exact + (jnp.log(jnp.maximum(n, 1).astype(jnp.float32) / max_exact)
                         / math.log(MAX_DISTANCE / max_exact)
                         * (NUM_BUCKETS - max_exact)).astype(jnp.int32)
    large = jnp.minimum(large, NUM_BUCKETS - 1)
    bucket = jnp.where(n < max_exact, n, large)
    val = jnp.zeros((t, t), jnp.float32)
    for b in range(NUM_BUCKETS):
        val = jnp.where(bucket == b, tab_ref[b, h], val)
    val = val - tab_ref[NUM_BUCKETS - 1, h]
    o_ref[...] = jnp.where(rel >= 0, val, MASK_VALUE)


def _bias_tiles(rel_bias, t):
    nh = rel_bias.shape[1]
    return pl.pallas_call(
        _bias_kernel,
        grid=(nh, 2),
        in_specs=[pl.BlockSpec(memory_space=pltpu.SMEM)],
        out_specs=pl.BlockSpec((None, None, t, t), lambda h, r: (h, r, 0, 0)),
        out_shape=jax.ShapeDtypeStruct((nh, 2, t, t), jnp.float32),
        compiler_params=_params("parallel", "parallel"),
        name="bias_tiles",
    )(rel_bias)


def _diff_attn_kernel(q_ref, k_ref, v_ref, bias_ref, lq_ref, lk_ref, g_ref, o_ref,
                      *, tq, lambda_init):
    n = q_ref.shape[0] // tq
    d = HEAD_DIM
    lq = lq_ref[...]
    lk = lk_ref[...]
    dots = jnp.sum(lq * lk, axis=-1, keepdims=True)
    lam = jnp.exp(dots[0:1]) - jnp.exp(dots[1:2]) + lambda_init

    def scores(i):
        lo = i * tq
        maps = []
        for idx in range(2):
            cols = slice(idx * d, (idx + 1) * d)
            q = q_ref[lo:lo + tq, cols]
            parts = []
            if i > 1:
                parts.append((_nt_dot(q, k_ref[:lo - tq, cols]), 0, lo - tq))
            if i > 0:
                parts.append((_nt_dot(q, k_ref[lo - tq:lo, cols]) + bias_ref[1], lo - tq, lo))
            parts.append((_nt_dot(q, k_ref[lo:lo + tq, cols]) + bias_ref[0], lo, lo + tq))
            maps.append(parts)
        return maps

    def softmax_pv(parts):
        m = functools.reduce(jnp.maximum,
                             [jnp.max(s, axis=-1, keepdims=True) for s, _, _ in parts])
        l = None
        acc = None
        for s, lo, hi in parts:
            p = jnp.exp(s - m)
            ps = jnp.sum(p, axis=-1, keepdims=True)
            pv = jnp.dot(p.astype(v_ref.dtype), v_ref[lo:hi, :],
                         preferred_element_type=jnp.float32)
            l = ps if l is None else l + ps
            acc = pv if acc is None else acc + pv
        return acc / l

    def finish(i, maps):
        o = softmax_pv(maps[0]) - lam * softmax_pv(maps[1])
        o = o * lax.rsqrt(jnp.mean(o * o, axis=-1, keepdims=True) + RMS_EPS) * g_ref[...]
        o_ref[i * tq:(i + 1) * tq, :] = (o * (1.0 - lambda_init)).astype(o_ref.dtype)

    nxt = scores(0)
    for i in range(n):
        cur = nxt
        if i + 1 < n:
            nxt = scores(i + 1)
        finish(i, cur)


def _diff_attention(proj, bias, lam_q, lam_k, subln_g, bsz, s, lambda_init):
    t = ATTN_TQ
    h, d2 = DIFF_HEADS, 2 * HEAD_DIM
    kern = functools.partial(_diff_attn_kernel, tq=t, lambda_init=lambda_init)
    return pl.pallas_call(
        kern,
        grid=(bsz, h),
        in_specs=[pl.BlockSpec((None, s, d2), lambda b, hh: (b, 0, hh)),
                  pl.BlockSpec((None, s, d2), lambda b, hh: (b, 0, h + hh)),
                  pl.BlockSpec((None, s, d2), lambda b, hh: (b, 0, 2 * h + hh)),
                  pl.BlockSpec((None, 2, t, t), lambda b, hh: (hh, 0, 0, 0)),
                  pl.BlockSpec((2, HEAD_DIM), lambda b, hh: (0, 0)),
                  pl.BlockSpec((2, HEAD_DIM), lambda b, hh: (0, 0)),
                  pl.BlockSpec((1, d2), lambda b, hh: (0, 0))],
        out_specs=pl.BlockSpec((None, s, d2), lambda b, hh: (b, 0, hh)),
        out_shape=jax.ShapeDtypeStruct((bsz, s, h * d2), jnp.bfloat16),
        compiler_params=_params("parallel", "parallel"),
        name="diff_attn",
    )(proj, proj, proj, bias, lam_q, lam_k, subln_g.reshape(1, d2))


def _out_ln_kernel(o_ref, z_ref, x_ref, w_ref, g_ref, b_ref, y_ref, *, alpha):
    o = o_ref[...].astype(jnp.float32)
    z = z_ref[...].astype(jnp.float32)
    gated = o * (z / (1.0 + jnp.exp(-z)))
    y = jnp.dot(gated.astype(jnp.bfloat16), w_ref[...], preferred_element_type=jnp.float32)
    v = alpha * x_ref[...] + y
    mu = jnp.mean(v, axis=-1, keepdims=True)
    vc = v - mu
    var = jnp.mean(vc * vc, axis=-1, keepdims=True)
    y_ref[...] = vc * lax.rsqrt(var + LN_EPS) * g_ref[...] + b_ref[...]


def _out_ln(o2d, proj2d, x2d, w_bf16, g, b, alpha):
    m, d = x2d.shape
    br = o2d.shape[1]
    tm = OUT_TM
    z_block = proj2d.shape[1] // br - 1
    kern = functools.partial(_out_ln_kernel, alpha=alpha)
    return pl.pallas_call(
        kern,
        grid=(m // tm,),
        in_specs=[pl.BlockSpec((tm, br), lambda i: (i, 0)),
                  pl.BlockSpec((tm, br), lambda i: (i, z_block)),
                  pl.BlockSpec((tm, d), lambda i: (i, 0)),
                  pl.BlockSpec((br, d), lambda i: (0, 0)),
                  pl.BlockSpec((1, d), lambda i: (0, 0)),
                  pl.BlockSpec((1, d), lambda i: (0, 0))],
        out_specs=pl.BlockSpec((tm, d), lambda i: (i, 0)),
        out_shape=jax.ShapeDtypeStruct((m, d), jnp.float32),
        compiler_params=_params("parallel"),
        name="out_ln",
    )(o2d, proj2d, x2d, w_bf16, g.reshape(1, d), b.reshape(1, d))


def kernel(x, fox_w_in, fox_b_f, fox_w_out, diff_w_in, diff_lam_q, diff_lam_k,
           diff_subln_g, diff_w_out, rel_bias, ln_g, ln_b):
    bsz, s, d = x.shape
    depth = ln_g.shape[0]
    branch = fox_w_out.shape[1]
    alpha = (2 * depth) ** 0.25
    scale = HEAD_DIM ** -0.5
    m = bsz * s
    bf16 = jnp.bfloat16

    x2d = x.reshape(m, d)
    for i in range(depth):
        j = i // 2
        if i % 2 == 0:
            w_in = fox_w_in[j]
            proj = _project(x2d, w_in[:, :4 * branch].astype(bf16), branch, scale)
            c = _forget_cumsum(x2d.reshape(bsz, s, d), w_in[:, 4 * branch:].T, fox_b_f[j])
            o = _fox_attention(proj.reshape(bsz, s, 4 * branch),
                               c.reshape(bsz, FOX_HEADS, 1, s), bsz, s)
            w_out = fox_w_out[j]
        else:
            lambda_init = 0.8 - 0.6 * math.exp(-0.3 * i)
            proj = _project(x2d, diff_w_in[j].astype(bf16), branch, scale)
            bias = _bias_tiles(rel_bias, ATTN_TQ)
            o = _diff_attention(proj.reshape(bsz, s, 4 * branch), bias, diff_lam_q[j],
                                diff_lam_k[j], diff_subln_g[j], bsz, s, lambda_init)
            w_out = diff_w_out[j]
        x2d = _out_ln(o.reshape(m, branch), proj, x2d, w_out.astype(bf16),
                      ln_g[i], ln_b[i], alpha)
    return x2d.reshape(bsz, s, d)
```

```python
import functools
import math

import jax
import jax.numpy as jnp
from jax import lax
from jax.experimental import pallas as pl
from jax.experimental.pallas import tpu as pltpu

LANES = 128
HEAD_DIM = 128
FOX_HEADS = 16
DIFF_HEADS = 8
NUM_BUCKETS = 32
MAX_DISTANCE = 128
LN_EPS = 1e-5
RMS_EPS = 1e-5
MASK_VALUE = -1e30
LOG2E = math.log2(math.e)

V7X_VMEM_LIMIT = 56 * 1024 * 1024

PROJ_TM = 1024
PROJ_TN = 1024
FORGET_TS = 512
ATTN_TQ = 256
SOFTMAX_ROWS = 16
OUT_TM = 256

_NT = (((1,), (1,)), ((), ()))


def _params(*sem):
    return pltpu.CompilerParams(dimension_semantics=sem, vmem_limit_bytes=V7X_VMEM_LIMIT)


def _proj_kernel(x_ref, w_ref, o_ref, wb_ref, *, n_scaled_tiles, scale):
    j = pl.program_id(0)

    @pl.when(pl.program_id(1) == 0)
    def _():
        wb_ref[...] = w_ref[...].astype(wb_ref.dtype)

    acc = jnp.dot(x_ref[...], wb_ref[...], preferred_element_type=jnp.float32)
    s = jnp.where(j < n_scaled_tiles, jnp.float32(scale), jnp.float32(1.0))
    o_ref[...] = (acc * s).astype(o_ref.dtype)


def _project(xb2d, w, n, q_cols, scale):
    m, k = xb2d.shape
    tm, tn = PROJ_TM, PROJ_TN
    kern = functools.partial(_proj_kernel, n_scaled_tiles=q_cols // tn, scale=scale)
    return pl.pallas_call(
        kern,
        grid=(n // tn, m // tm),
        in_specs=[pl.BlockSpec((tm, k), lambda j, i: (i, 0)),
                  pl.BlockSpec((k, tn), lambda j, i: (0, j))],
        out_specs=pl.BlockSpec((tm, tn), lambda j, i: (i, j)),
        out_shape=jax.ShapeDtypeStruct((m, n), jnp.bfloat16),
        scratch_shapes=[pltpu.VMEM((k, tn), jnp.bfloat16)],
        compiler_params=_params("parallel", "arbitrary"),
        name="in_proj",
    )(xb2d, w)


def _split3(a):
    p1 = a.astype(jnp.bfloat16)
    r1 = a - p1.astype(jnp.float32)
    p2 = r1.astype(jnp.bfloat16)
    p3 = (r1 - p2.astype(jnp.float32)).astype(jnp.bfloat16)
    return p1, p2, p3


def _forget_kernel(x_ref, wt_ref, b_ref, c_ref, xb_ref, carry_ref):
    t = pl.program_id(1)
    ts = x_ref.shape[0]

    @pl.when(t == 0)
    def _():
        carry_ref[...] = jnp.zeros_like(carry_ref)

    x = x_ref[...]
    w = wt_ref[...]
    xh = x.astype(jnp.bfloat16)
    xb_ref[...] = xh
    xl = (x - xh.astype(jnp.float32)).astype(jnp.bfloat16)
    wh = w.astype(jnp.bfloat16)
    wl = (w - wh.astype(jnp.float32)).astype(jnp.bfloat16)
    dot = functools.partial(lax.dot_general, dimension_numbers=_NT,
                            preferred_element_type=jnp.float32)
    f = dot(wh, xh) + (dot(wl, xh) + dot(wh, xl))
    z = f + b_ref[...]
    logf = jnp.minimum(z, 0.0) - jnp.log1p(jnp.exp(-jnp.abs(z)))
    row = lax.broadcasted_iota(jnp.int32, (ts, ts), 0)
    col = lax.broadcasted_iota(jnp.int32, (ts, ts), 1)
    tri = (row <= col).astype(jnp.bfloat16)
    p1, p2, p3 = _split3(logf)
    csum = (jnp.dot(p1, tri, preferred_element_type=jnp.float32)
            + jnp.dot(p2, tri, preferred_element_type=jnp.float32)
            + jnp.dot(p3, tri, preferred_element_type=jnp.float32))
    c = csum + carry_ref[...]
    c_ref[...] = c * LOG2E
    carry_ref[...] = c[:, ts - 1:ts]


def _forget_cumsum(x, wf_t, b_f):
    bsz, s, d = x.shape
    h = wf_t.shape[0]
    ts = FORGET_TS
    return pl.pallas_call(
        _forget_kernel,
        grid=(bsz, s // ts),
        in_specs=[pl.BlockSpec((None, ts, d), lambda b, t: (b, t, 0)),
                  pl.BlockSpec((h, d), lambda b, t: (0, 0)),
                  pl.BlockSpec((h, 1), lambda b, t: (0, 0))],
        out_specs=[pl.BlockSpec((None, h, ts), lambda b, t: (b, 0, t)),
                   pl.BlockSpec((None, ts, d), lambda b, t: (b, t, 0))],
        out_shape=[jax.ShapeDtypeStruct((bsz, h, s), jnp.float32),
                   jax.ShapeDtypeStruct((bsz, s, d), jnp.bfloat16)],
        scratch_shapes=[pltpu.VMEM((h, 1), jnp.float32)],
        compiler_params=_params("parallel", "arbitrary"),
        name="forget_cumsum",
    )(x, wf_t, b_f.reshape(h, 1))


def _causal_tile_mask(t):
    row = lax.broadcasted_iota(jnp.int32, (t, t), 0)
    col = lax.broadcasted_iota(jnp.int32, (t, t), 1)
    return col <= row


def _nt_dot(a, b):
    return lax.dot_general(a, b, _NT, preferred_element_type=jnp.float32)


def _row_reduce(parts, combine, lane_reduce):
    tiles = [s[:, j:j + LANES] for s in parts for j in range(0, s.shape[1], LANES)]
    while len(tiles) > 1:
        tiles = [combine(tiles[j], tiles[j + 1]) if j + 1 < len(tiles) else tiles[j]
                 for j in range(0, len(tiles), 2)]
    return lane_reduce(tiles[0], axis=-1, keepdims=True)


def _exp2_softmax_rows(parts, dtype, with_sum):
    rows = parts[0].shape[0]
    out, sums = [], []
    for r in range(0, rows, SOFTMAX_ROWS):
        chunk = [s[r:r + SOFTMAX_ROWS] for s in parts]
        m = _row_reduce(chunk, jnp.maximum, jnp.max)
        ps = [jnp.exp2(s - m) for s in chunk]
        if with_sum:
            sums.append(_row_reduce(ps, jnp.add, jnp.sum))
        out.append(jnp.concatenate([p.astype(dtype) for p in ps], axis=1))
    p = jnp.concatenate(out, axis=0)
    return (p, jnp.concatenate(sums, axis=0)) if with_sum else p


def _emit_pipelined(n, scores, softmax, output):
    s = {0: scores(0)}
    if n > 1:
        s[1] = scores(1)
    a = {0: softmax(0, s.pop(0))}
    for t in range(n):
        if t + 2 < n:
            s[t + 2] = scores(t + 2)
        if t + 1 < n:
            a[t + 1] = softmax(t + 1, s.pop(t + 1))
        output(t, a.pop(t))


def _fox_attn_kernel(q_ref, k_ref, v_ref, c_ref, o_ref, va_ref, *, tq):
    n = q_ref.shape[0] // tq
    d = v_ref.shape[1]
    mask = _causal_tile_mask(tq)
    va_ref[:, :d] = v_ref[...]
    va_ref[:, d:] = jnp.ones((va_ref.shape[0], d), va_ref.dtype)

    def scores(i):
        lo = i * tq
        s = _nt_dot(q_ref[lo:lo + tq, :], k_ref[:lo + tq, :]) - c_ref[:, :lo + tq]
        diag = jnp.where(mask, s[:, lo:], MASK_VALUE)
        return [s[:, :lo], diag] if i > 0 else [diag]

    def softmax(i, parts):
        return _exp2_softmax_rows(parts, va_ref.dtype, with_sum=False)

    def output(i, p):
        hi = (i + 1) * tq
        pv = jnp.dot(p, va_ref[:hi, :], preferred_element_type=jnp.float32)
        o_ref[hi - tq:hi, :] = (pv[:, :d] / pv[:, d:]).astype(o_ref.dtype)

    _emit_pipelined(n, scores, softmax, output)


def _fox_attention(proj, c, bsz, s):
    h, d = FOX_HEADS, HEAD_DIM
    kern = functools.partial(_fox_attn_kernel, tq=ATTN_TQ)
    return pl.pallas_call(
        kern,
        grid=(bsz, h),
        in_specs=[pl.BlockSpec((None, s, d), lambda b, hh: (b, 0, hh)),
                  pl.BlockSpec((None, s, d), lambda b, hh: (b, 0, h + hh)),
                  pl.BlockSpec((None, s, d), lambda b, hh: (b, 0, 2 * h + hh)),
                  pl.BlockSpec((None, None, 1, s), lambda b, hh: (b, hh, 0, 0))],
        out_specs=pl.BlockSpec((None, s, d), lambda b, hh: (b, 0, hh)),
        out_shape=jax.ShapeDtypeStruct((bsz, s, h * d), jnp.bfloat16),
        scratch_shapes=[pltpu.VMEM((s, 2 * d), jnp.bfloat16)],
        compiler_params=_params("parallel", "parallel"),
        name="fox_attn",
    )(proj, proj, proj, c)


def _bias_kernel(tab_ref, o_ref):
    h = pl.program_id(0)
    r = pl.program_id(1)
    t = o_ref.shape[0]
    dq = lax.broadcasted_iota(jnp.int32, (t, t), 0)
    dk = lax.broadcasted_iota(jnp.int32, (t, t), 1)
    rel = r * t + dq - dk
    n = jnp.maximum(rel, 0)
    max_exact = NUM_BUCKETS // 2
    large = max_exact + (jnp.log(jnp.maximum(n, 1).astype(jnp.float32) / max_exact)
                         / math.log(MAX_DISTANCE / max_exact)
                         * (NUM_BUCKETS - max_exact)).astype(jnp.int32)
    large = jnp.minimum(large, NUM_BUCKETS - 1)
    bucket = jnp.where(n < max_exact, n, large)
    val = jnp.zeros((t, t), jnp.float32)
    for b in range(NUM_BUCKETS):
        val = jnp.where(bucket == b, tab_ref[b, h], val)
    val = (val - tab_ref[NUM_BUCKETS - 1, h]) * LOG2E
    o_ref[...] = jnp.where(rel >= 0, val, MASK_VALUE)


def _bias_tiles(rel_bias, t):
    nh = rel_bias.shape[1]
    return pl.pallas_call(
        _bias_kernel,
        grid=(nh, 2),
        in_specs=[pl.BlockSpec(memory_space=pltpu.SMEM)],
        out_specs=pl.BlockSpec((None, None, t, t), lambda h, r: (h, r, 0, 0)),
        out_shape=jax.ShapeDtypeStruct((nh, 2, t, t), jnp.float32),
        compiler_params=_params("parallel", "parallel"),
        name="bias_tiles",
    )(rel_bias)


def _diff_attn_kernel(q_ref, k_ref, v_ref, bias_ref, lq_ref, lk_ref, g_ref, o_ref,
                      *, tq, lambda_init):
    n = q_ref.shape[0] // tq
    d = HEAD_DIM
    lq = lq_ref[...]
    lk = lk_ref[...]
    dots = jnp.sum(lq * lk, axis=-1, keepdims=True)
    lam = jnp.exp(dots[0:1]) - jnp.exp(dots[1:2]) + lambda_init

    def scores(i):
        lo = i * tq
        maps = []
        for idx in range(2):
            cols = slice(idx * d, (idx + 1) * d)
            s = _nt_dot(q_ref[lo:lo + tq, cols], k_ref[:lo + tq, cols])
            parts = [s[:, :lo - tq]] if i > 1 else []
            if i > 0:
                parts.append(s[:, lo - tq:lo] + bias_ref[1])
            parts.append(s[:, lo:] + bias_ref[0])
            maps.append(parts)
        return maps

    def softmax(i, maps):
        p1, l1 = _exp2_softmax_rows(maps[0], v_ref.dtype, with_sum=True)
        p2, l2 = _exp2_softmax_rows(maps[1], v_ref.dtype, with_sum=True)
        return jnp.concatenate([p1, p2], axis=0), 1.0 / l1, lam / l2

    def output(i, args):
        p, r1, r2 = args
        hi = (i + 1) * tq
        pv = jnp.dot(p, v_ref[:hi, :], preferred_element_type=jnp.float32)
        o = pv[:tq] * r1 - pv[tq:] * r2
        o = o * lax.rsqrt(jnp.mean(o * o, axis=-1, keepdims=True) + RMS_EPS) * g_ref[...]
        o_ref[hi - tq:hi, :] = (o * (1.0 - lambda_init)).astype(o_ref.dtype)

    _emit_pipelined(n, scores, softmax, output)


def _diff_attention(proj, bias, lam_q, lam_k, subln_g, bsz, s, lambda_init):
    t = ATTN_TQ
    h, d2 = DIFF_HEADS, 2 * HEAD_DIM
    kern = functools.partial(_diff_attn_kernel, tq=t, lambda_init=lambda_init)
    return pl.pallas_call(
        kern,
        grid=(bsz, h),
        in_specs=[pl.BlockSpec((None, s, d2), lambda b, hh: (b, 0, hh)),
                  pl.BlockSpec((None, s, d2), lambda b, hh: (b, 0, h + hh)),
                  pl.BlockSpec((None, s, d2), lambda b, hh: (b, 0, 2 * h + hh)),
                  pl.BlockSpec((None, 2, t, t), lambda b, hh: (hh, 0, 0, 0)),
                  pl.BlockSpec((2, HEAD_DIM), lambda b, hh: (0, 0)),
                  pl.BlockSpec((2, HEAD_DIM), lambda b, hh: (0, 0)),
                  pl.BlockSpec((1, d2), lambda b, hh: (0, 0))],
        out_specs=pl.BlockSpec((None, s, d2), lambda b, hh: (b, 0, hh)),
        out_shape=jax.ShapeDtypeStruct((bsz, s, h * d2), jnp.bfloat16),
        compiler_params=_params("parallel", "parallel"),
        name="diff_attn",
    )(proj, proj, proj, bias, lam_q, lam_k, subln_g.reshape(1, d2))


def _out_ln_kernel(o_ref, z_ref, x_ref, w_ref, g_ref, b_ref, *refs, alpha, n, emit_bf16):
    if emit_bf16:
        y_ref, yb_ref, acc0_ref, acc1_ref = refs
    else:
        (y_ref, acc0_ref, acc1_ref), yb_ref = refs, None
    accs = (acc0_ref, acc1_ref)
    i = pl.program_id(0)

    def matmul_into(acc_ref):
        o = o_ref[...].astype(jnp.float32)
        z = z_ref[...].astype(jnp.float32)
        gated = o * (z / (1.0 + jnp.exp(-z)))
        acc_ref[...] = jnp.dot(gated.astype(w_ref.dtype), w_ref[...],
                               preferred_element_type=jnp.float32)

    def layer_norm_from(acc_ref):
        v = alpha * x_ref[...] + acc_ref[...]
        mu = jnp.mean(v, axis=-1, keepdims=True)
        vc = v - mu
        var = jnp.mean(vc * vc, axis=-1, keepdims=True)
        y = vc * lax.rsqrt(var + LN_EPS) * g_ref[...] + b_ref[...]
        y_ref[...] = y
        if emit_bf16:
            yb_ref[...] = y.astype(yb_ref.dtype)

    @pl.when(i == 0)
    def _():
        matmul_into(accs[0])

    for parity in range(2):
        @pl.when((i > 0) & (i < n) & (i % 2 == parity))
        def _():
            matmul_into(accs[parity])
            layer_norm_from(accs[1 - parity])

    @pl.when(i == n)
    def _():
        layer_norm_from(accs[(n - 1) % 2])


def _out_ln(o2d, proj2d, x2d, w_bf16, g, b, alpha, emit_bf16):
    m, d = x2d.shape
    br = o2d.shape[1]
    tm = OUT_TM
    n = m // tm
    z_block = proj2d.shape[1] // br - 1
    kern = functools.partial(_out_ln_kernel, alpha=alpha, n=n, emit_bf16=emit_bf16)
    cur = lambda i: jnp.minimum(i, n - 1)
    prev = lambda i: jnp.maximum(i - 1, 0)
    out_specs = [pl.BlockSpec((tm, d), lambda i: (prev(i), 0))]
    out_shape = [jax.ShapeDtypeStruct((m, d), jnp.float32)]
    if emit_bf16:
        out_specs.append(pl.BlockSpec((tm, d), lambda i: (prev(i), 0)))
        out_shape.append(jax.ShapeDtypeStruct((m, d), jnp.bfloat16))
    return pl.pallas_call(
        kern,
        grid=(n + 1,),
        in_specs=[pl.BlockSpec((tm, br), lambda i: (cur(i), 0)),
                  pl.BlockSpec((tm, br), lambda i: (cur(i), z_block)),
                  pl.BlockSpec((tm, d), lambda i: (prev(i), 0)),
                  pl.BlockSpec((br, d), lambda i: (0, 0)),
                  pl.BlockSpec((1, d), lambda i: (0, 0)),
                  pl.BlockSpec((1, d), lambda i: (0, 0))],
        out_specs=out_specs,
        out_shape=out_shape,
        scratch_shapes=[pltpu.VMEM((tm, d), jnp.float32), pltpu.VMEM((tm, d), jnp.float32)],
        compiler_params=_params("arbitrary"),
        name="out_ln",
    )(o2d, proj2d, x2d, w_bf16, g.reshape(1, d), b.reshape(1, d))


def kernel(x, fox_w_in, fox_b_f, fox_w_out, diff_w_in, diff_lam_q, diff_lam_k,
           diff_subln_g, diff_w_out, rel_bias, ln_g, ln_b):
    bsz, s, d = x.shape
    depth = ln_g.shape[0]
    branch = fox_w_out.shape[1]
    alpha = (2 * depth) ** 0.25
    scale = HEAD_DIM ** -0.5 * LOG2E
    m = bsz * s
    bf16 = jnp.bfloat16

    x2d = x.reshape(m, d)
    xb2d = None
    for i in range(depth):
        j = i // 2
        if i % 2 == 0:
            w_in = fox_w_in[j]
            c, xb = _forget_cumsum(x2d.reshape(bsz, s, d), w_in[:, 4 * branch:].T, fox_b_f[j])
            xb2d = xb.reshape(m, d)
            proj = _project(xb2d, w_in, 4 * branch, branch, scale)
            o = _fox_attention(proj.reshape(bsz, s, 4 * branch),
                               c.reshape(bsz, FOX_HEADS, 1, s), bsz, s)
            w_out = fox_w_out[j]
        else:
            lambda_init = 0.8 - 0.6 * math.exp(-0.3 * i)
            proj = _project(xb2d, diff_w_in[j], 4 * branch, branch, scale)
            bias = _bias_tiles(rel_bias, ATTN_TQ)
            o = _diff_attention(proj.reshape(bsz, s, 4 * branch), bias, diff_lam_q[j],
                                diff_lam_k[j], diff_subln_g[j], bsz, s, lambda_init)
            w_out = diff_w_out[j]
        next_is_diff = i + 1 < depth and (i + 1) % 2 == 1
        outs = _out_ln(o.reshape(m, branch), proj, x2d, w_out.astype(bf16),
                       ln_g[i], ln_b[i], alpha, emit_bf16=next_is_diff)
        x2d = outs[0]
        xb2d = outs[1] if next_is_diff else None
    return x2d.reshape(bsz, s, d)
```

```python
import functools
import math

import jax
import jax.numpy as jnp
from jax import lax
from jax.experimental import pallas as pl
from jax.experimental.pallas import tpu as pltpu

LANES = 128
HEAD_DIM = 128
FOX_HEADS = 16
DIFF_HEADS = 8
NUM_BUCKETS = 32
MAX_DISTANCE = 128
LN_EPS = 1e-5
RMS_EPS = 1e-5
MASK_VALUE = -1e30
LOG2E = math.log2(math.e)

V7X_VMEM_LIMIT = 56 * 1024 * 1024

PROJ_TM = 1024
PROJ_TN = 1024
FORGET_TS = 512
ATTN_TQ = 256
SOFTMAX_ROWS = 16
OUT_TM = 512

_NT = (((1,), (1,)), ((), ()))


def _nt_dot(a, b):
    return lax.dot_general(a, b, _NT, preferred_element_type=jnp.float32)


def _params(*sem):
    return pltpu.CompilerParams(dimension_semantics=sem, vmem_limit_bytes=V7X_VMEM_LIMIT)


def _proj_kernel(x_ref, w_ref, o_ref, wb_ref, *, n_scaled_tiles, scale, w_transposed):
    j = pl.program_id(0)

    @pl.when(pl.program_id(1) == 0)
    def _():
        wb_ref[...] = w_ref[...].astype(wb_ref.dtype)

    if w_transposed:
        acc = _nt_dot(x_ref[...], wb_ref[...])
    else:
        acc = jnp.dot(x_ref[...], wb_ref[...], preferred_element_type=jnp.float32)
    s = jnp.where(j < n_scaled_tiles, jnp.float32(scale), jnp.float32(1.0))
    o_ref[...] = (acc * s).astype(o_ref.dtype)


def _project(xb2d, w, n, q_cols, scale, w_transposed):
    m, k = xb2d.shape
    tm, tn = PROJ_TM, PROJ_TN
    kern = functools.partial(_proj_kernel, n_scaled_tiles=q_cols // tn, scale=scale,
                             w_transposed=w_transposed)
    if w_transposed:
        w_spec = pl.BlockSpec((tn, k), lambda j, i: (j, 0))
        wb_shape = (tn, k)
    else:
        w_spec = pl.BlockSpec((k, tn), lambda j, i: (0, j))
        wb_shape = (k, tn)
    return pl.pallas_call(
        kern,
        grid=(n // tn, m // tm),
        in_specs=[pl.BlockSpec((tm, k), lambda j, i: (i, 0)), w_spec],
        out_specs=pl.BlockSpec((tm, tn), lambda j, i: (i, j)),
        out_shape=jax.ShapeDtypeStruct((m, n), jnp.bfloat16),
        scratch_shapes=[pltpu.VMEM(wb_shape, jnp.bfloat16)],
        compiler_params=_params("parallel", "arbitrary"),
        name="in_proj",
    )(xb2d, w)


def _split3(a):
    p1 = a.astype(jnp.bfloat16)
    r1 = a - p1.astype(jnp.float32)
    p2 = r1.astype(jnp.bfloat16)
    p3 = (r1 - p2.astype(jnp.float32)).astype(jnp.bfloat16)
    return p1, p2, p3


def _forget_kernel(x_ref, wt_ref, b_ref, c_ref, xb_ref, carry_ref):
    t = pl.program_id(1)
    ts = x_ref.shape[0]

    @pl.when(t == 0)
    def _():
        carry_ref[...] = jnp.zeros_like(carry_ref)

    xb = x_ref[...].astype(jnp.bfloat16)
    xb_ref[...] = xb
    f = _nt_dot(wt_ref[...].astype(jnp.bfloat16), xb)
    z = f + b_ref[...]
    logf = jnp.minimum(z, 0.0) - jnp.log1p(jnp.exp(-jnp.abs(z)))
    row = lax.broadcasted_iota(jnp.int32, (ts, ts), 0)
    col = lax.broadcasted_iota(jnp.int32, (ts, ts), 1)
    tri = (row <= col).astype(jnp.bfloat16)
    p1, p2, p3 = _split3(logf)
    csum = (jnp.dot(p1, tri, preferred_element_type=jnp.float32)
            + jnp.dot(p2, tri, preferred_element_type=jnp.float32)
            + jnp.dot(p3, tri, preferred_element_type=jnp.float32))
    c = csum + carry_ref[...]
    c_ref[...] = c * LOG2E
    carry_ref[...] = c[:, ts - 1:ts]


def _forget_cumsum(x, wf_t, b_f):
    bsz, s, d = x.shape
    h = wf_t.shape[0]
    ts = FORGET_TS
    return pl.pallas_call(
        _forget_kernel,
        grid=(bsz, s // ts),
        in_specs=[pl.BlockSpec((None, ts, d), lambda b, t: (b, t, 0)),
                  pl.BlockSpec((h, d), lambda b, t: (0, 0)),
                  pl.BlockSpec((h, 1), lambda b, t: (0, 0))],
        out_specs=[pl.BlockSpec((None, h, ts), lambda b, t: (b, 0, t)),
                   pl.BlockSpec((None, ts, d), lambda b, t: (b, t, 0))],
        out_shape=[jax.ShapeDtypeStruct((bsz, h, s), jnp.float32),
                   jax.ShapeDtypeStruct((bsz, s, d), jnp.bfloat16)],
        scratch_shapes=[pltpu.VMEM((h, 1), jnp.float32)],
        compiler_params=_params("parallel", "arbitrary"),
        name="forget_cumsum",
    )(x, wf_t, b_f.reshape(h, 1))


def _causal_tile_mask(t):
    row = lax.broadcasted_iota(jnp.int32, (t, t), 0)
    col = lax.broadcasted_iota(jnp.int32, (t, t), 1)
    return col <= row


def _row_reduce(parts, combine, lane_reduce):
    tiles = [s[:, j:j + LANES] for s in parts for j in range(0, s.shape[1], LANES)]
    while len(tiles) > 1:
        tiles = [combine(tiles[j], tiles[j + 1]) if j + 1 < len(tiles) else tiles[j]
                 for j in range(0, len(tiles), 2)]
    return lane_reduce(tiles[0], axis=-1, keepdims=True)


def _exp2_softmax_rows(parts, dtype, with_sum):
    rows = parts[0].shape[0]
    out, sums = [], []
    for r in range(0, rows, SOFTMAX_ROWS):
        chunk = [s[r:r + SOFTMAX_ROWS] for s in parts]
        m = _row_reduce(chunk, jnp.maximum, jnp.max)
        ps = [jnp.exp2(s - m) for s in chunk]
        if with_sum:
            sums.append(_row_reduce(ps, jnp.add, jnp.sum))
        out.append(jnp.concatenate([p.astype(dtype) for p in ps], axis=1))
    p = jnp.concatenate(out, axis=0)
    return (p, jnp.concatenate(sums, axis=0)) if with_sum else p


def _emit_pipelined(n, scores, softmax, output):
    s = {0: scores(0)}
    if n > 1:
        s[1] = scores(1)
    a = {0: softmax(0, s.pop(0))}
    for t in range(n):
        if t + 2 < n:
            s[t + 2] = scores(t + 2)
        if t + 1 < n:
            a[t + 1] = softmax(t + 1, s.pop(t + 1))
        output(t, a.pop(t))


def _fox_attn_kernel(q_ref, k_ref, v_ref, c_ref, o_ref, va_ref, *, tq):
    n = q_ref.shape[0] // tq
    d = v_ref.shape[1]
    mask = _causal_tile_mask(tq)
    va_ref[:, :d] = v_ref[...]
    va_ref[:, d:] = jnp.ones((va_ref.shape[0], d), va_ref.dtype)

    def scores(i):
        lo = i * tq
        s = _nt_dot(q_ref[lo:lo + tq, :], k_ref[:lo + tq, :]) - c_ref[:, :lo + tq]
        diag = jnp.where(mask, s[:, lo:], MASK_VALUE)
        return [s[:, :lo], diag] if i > 0 else [diag]

    def softmax(i, parts):
        return _exp2_softmax_rows(parts, va_ref.dtype, with_sum=False)

    def output(i, p):
        hi = (i + 1) * tq
        pv = jnp.dot(p, va_ref[:hi, :], preferred_element_type=jnp.float32)
        o_ref[hi - tq:hi, :] = (pv[:, :d] / pv[:, d:]).astype(o_ref.dtype)

    _emit_pipelined(n, scores, softmax, output)


def _fox_attention(proj, c, bsz, s):
    h, d = FOX_HEADS, HEAD_DIM
    kern = functools.partial(_fox_attn_kernel, tq=ATTN_TQ)
    return pl.pallas_call(
        kern,
        grid=(bsz, h),
        in_specs=[pl.BlockSpec((None, s, d), lambda b, hh: (b, 0, hh)),
                  pl.BlockSpec((None, s, d), lambda b, hh: (b, 0, h + hh)),
                  pl.BlockSpec((None, s, d), lambda b, hh: (b, 0, 2 * h + hh)),
                  pl.BlockSpec((None, None, 1, s), lambda b, hh: (b, hh, 0, 0))],
        out_specs=pl.BlockSpec((None, s, d), lambda b, hh: (b, 0, hh)),
        out_shape=jax.ShapeDtypeStruct((bsz, s, h * d), jnp.bfloat16),
        scratch_shapes=[pltpu.VMEM((s, 2 * d), jnp.bfloat16)],
        compiler_params=_params("parallel", "parallel"),
        name="fox_attn",
    )(proj, proj, proj, c)


def _bias_kernel(tab_ref, o_ref):
    h = pl.program_id(0)
    r = pl.program_id(1)
    t = o_ref.shape[0]
    dq = lax.broadcasted_iota(jnp.int32, (t, t), 0)
    dk = lax.broadcasted_iota(jnp.int32, (t, t), 1)
    rel = r * t + dq - dk
    n = jnp.maximum(rel, 0)
    max_exact = NUM_BUCKETS // 2
    large = max_exact + (jnp.log(jnp.maximum(n, 1).astype(jnp.float32) / max_exact)
                         / math.log(MAX_DISTANCE / max_exact)
                         * (NUM_BUCKETS - max_exact)).astype(jnp.int32)
    large = jnp.minimum(large, NUM_BUCKETS - 1)
    bucket = jnp.where(n < max_exact, n, large)
    val = jnp.zeros((t, t), jnp.float32)
    for b in range(NUM_BUCKETS):
        val = jnp.where(bucket == b, tab_ref[b, h], val)
    val = (val - tab_ref[NUM_BUCKETS - 1, h]) * LOG2E
    o_ref[...] = jnp.where(rel >= 0, val, MASK_VALUE)


def _bias_tiles(rel_bias, t):
    nh = rel_bias.shape[1]
    return pl.pallas_call(
        _bias_kernel,
        grid=(nh, 2),
        in_specs=[pl.BlockSpec(memory_space=pltpu.SMEM)],
        out_specs=pl.BlockSpec((None, None, t, t), lambda h, r: (h, r, 0, 0)),
        out_shape=jax.ShapeDtypeStruct((nh, 2, t, t), jnp.float32),
        compiler_params=_params("parallel", "parallel"),
        name="bias_tiles",
    )(rel_bias)


def _diff_attn_kernel(q_ref, k_ref, v_ref, bias_ref, lq_ref, lk_ref, g_ref, o_ref,
                      *, tq, lambda_init):
    n = q_ref.shape[0] // tq
    d = HEAD_DIM
    lq = lq_ref[...]
    lk = lk_ref[...]
    dots = jnp.sum(lq * lk, axis=-1, keepdims=True)
    lam = jnp.exp(dots[0:1]) - jnp.exp(dots[1:2]) + lambda_init

    def scores(i):
        lo = i * tq
        maps = []
        for idx in range(2):
            cols = slice(idx * d, (idx + 1) * d)
            s = _nt_dot(q_ref[lo:lo + tq, cols], k_ref[:lo + tq, cols])
            parts = [s[:, :lo - tq]] if i > 1 else []
            if i > 0:
                parts.append(s[:, lo - tq:lo] + bias_ref[1])
            parts.append(s[:, lo:] + bias_ref[0])
            maps.append(parts)
        return maps

    def softmax(i, maps):
        p1, l1 = _exp2_softmax_rows(maps[0], v_ref.dtype, with_sum=True)
        p2, l2 = _exp2_softmax_rows(maps[1], v_ref.dtype, with_sum=True)
        return jnp.concatenate([p1, p2], axis=0), 1.0 / l1, lam / l2

    def output(i, args):
        p, r1, r2 = args
        hi = (i + 1) * tq
        pv = jnp.dot(p, v_ref[:hi, :], preferred_element_type=jnp.float32)
        o = pv[:tq] * r1 - pv[tq:] * r2
        o = o * lax.rsqrt(jnp.mean(o * o, axis=-1, keepdims=True) + RMS_EPS) * g_ref[...]
        o_ref[hi - tq:hi, :] = (o * (1.0 - lambda_init)).astype(o_ref.dtype)

    _emit_pipelined(n, scores, softmax, output)


def _diff_attention(proj, bias, lam_q, lam_k, subln_g, bsz, s, lambda_init):
    t = ATTN_TQ
    h, d2 = DIFF_HEADS, 2 * HEAD_DIM
    kern = functools.partial(_diff_attn_kernel, tq=t, lambda_init=lambda_init)
    return pl.pallas_call(
        kern,
        grid=(bsz, h),
        in_specs=[pl.BlockSpec((None, s, d2), lambda b, hh: (b, 0, hh)),
                  pl.BlockSpec((None, s, d2), lambda b, hh: (b, 0, h + hh)),
                  pl.BlockSpec((None, s, d2), lambda b, hh: (b, 0, 2 * h + hh)),
                  pl.BlockSpec((None, 2, t, t), lambda b, hh: (hh, 0, 0, 0)),
                  pl.BlockSpec((2, HEAD_DIM), lambda b, hh: (0, 0)),
                  pl.BlockSpec((2, HEAD_DIM), lambda b, hh: (0, 0)),
                  pl.BlockSpec((1, d2), lambda b, hh: (0, 0))],
        out_specs=pl.BlockSpec((None, s, d2), lambda b, hh: (b, 0, hh)),
        out_shape=jax.ShapeDtypeStruct((bsz, s, h * d2), jnp.bfloat16),
        compiler_params=_params("parallel", "parallel"),
        name="diff_attn",
    )(proj, proj, proj, bias, lam_q, lam_k, subln_g.reshape(1, d2))


def _out_ln_kernel(o_ref, z_ref, x_ref, w_ref, g_ref, b_ref, y_ref, *maybe_yb_ref, alpha):
    o = o_ref[...].astype(jnp.float32)
    z = z_ref[...].astype(jnp.float32)
    gated = o * (z / (1.0 + jnp.exp(-z)))
    acc = jnp.dot(gated.astype(w_ref.dtype), w_ref[...], preferred_element_type=jnp.float32)
    v = alpha * x_ref[...] + acc
    mu = jnp.mean(v, axis=-1, keepdims=True)
    vc = v - mu
    var = jnp.mean(vc * vc, axis=-1, keepdims=True)
    y = vc * lax.rsqrt(var + LN_EPS) * g_ref[...] + b_ref[...]
    y_ref[...] = y
    for yb_ref in maybe_yb_ref:
        yb_ref[...] = y.astype(yb_ref.dtype)


def _out_ln(o2d, proj2d, x2d, w_bf16, g, b, alpha, emit_bf16):
    m, d = x2d.shape
    br = o2d.shape[1]
    tm = OUT_TM
    z_block = proj2d.shape[1] // br - 1
    kern = functools.partial(_out_ln_kernel, alpha=alpha)
    resident = dict(pipeline_mode=pl.Buffered(1))
    out_specs = [pl.BlockSpec((tm, d), lambda i: (i, 0))]
    out_shape = [jax.ShapeDtypeStruct((m, d), jnp.float32)]
    if emit_bf16:
        out_specs.append(pl.BlockSpec((tm, d), lambda i: (i, 0)))
        out_shape.append(jax.ShapeDtypeStruct((m, d), jnp.bfloat16))
    return pl.pallas_call(
        kern,
        grid=(m // tm,),
        in_specs=[pl.BlockSpec((tm, br), lambda i: (i, 0)),
                  pl.BlockSpec((tm, br), lambda i: (i, z_block)),
                  pl.BlockSpec((tm, d), lambda i: (i, 0)),
                  pl.BlockSpec((br, d), lambda i: (0, 0), **resident),
                  pl.BlockSpec((1, d), lambda i: (0, 0), **resident),
                  pl.BlockSpec((1, d), lambda i: (0, 0), **resident)],
        out_specs=out_specs,
        out_shape=out_shape,
        compiler_params=_params("parallel"),
        name="out_ln",
    )(o2d, proj2d, x2d, w_bf16, g.reshape(1, d), b.reshape(1, d))


def kernel(x, fox_w_in, fox_b_f, fox_w_out, diff_w_in, diff_lam_q, diff_lam_k,
           diff_subln_g, diff_w_out, rel_bias, ln_g, ln_b):
    bsz, s, d = x.shape
    depth = ln_g.shape[0]
    branch = fox_w_out.shape[1]
    alpha = (2 * depth) ** 0.25
    scale = HEAD_DIM ** -0.5 * LOG2E
    m = bsz * s
    bf16 = jnp.bfloat16

    x2d = x.reshape(m, d)
    xb2d = None
    for i in range(depth):
        j = i // 2
        if i % 2 == 0:
            w_in_t = fox_w_in[j].T
            c, xb = _forget_cumsum(x2d.reshape(bsz, s, d), w_in_t[4 * branch:], fox_b_f[j])
            xb2d = xb.reshape(m, d)
            proj = _project(xb2d, w_in_t, 4 * branch, branch, scale, w_transposed=True)
            o = _fox_attention(proj.reshape(bsz, s, 4 * branch),
                               c.reshape(bsz, FOX_HEADS, 1, s), bsz, s)
            w_out = fox_w_out[j]
        else:
            lambda_init = 0.8 - 0.6 * math.exp(-0.3 * i)
            proj = _project(xb2d, diff_w_in[j], 4 * branch, branch, scale, w_transposed=False)
            bias = _bias_tiles(rel_bias, ATTN_TQ)
            o = _diff_attention(proj.reshape(bsz, s, 4 * branch), bias, diff_lam_q[j],
                                diff_lam_k[j], diff_subln_g[j], bsz, s, lambda_init)
            w_out = diff_w_out[j]
        next_is_diff = i + 1 < depth and (i + 1) % 2 == 1
        outs = _out_ln(o.reshape(m, branch), proj, x2d, w_out.astype(bf16),
                       ln_g[i], ln_b[i], alpha, emit_bf16=next_is_diff)
        x2d = outs[0]
        xb2d = outs[1] if next_is_diff else None
    return x2d.reshape(bsz, s, d)
```

```python
import functools
import math

import jax
import jax.numpy as jnp
from jax import lax
from jax.experimental import pallas as pl
from jax.experimental.pallas import tpu as pltpu

LANES = 128
HEAD_DIM = 128
FOX_HEADS = 16
DIFF_HEADS = 8
NUM_BUCKETS = 32
MAX_DISTANCE = 128
LN_EPS = 1e-5
RMS_EPS = 1e-5
MASK_VALUE = -1e30
LOG2E = math.log2(math.e)

V7X_VMEM_LIMIT = 56 * 1024 * 1024

PROJ_TM = 1024
PROJ_TN = 1024
FORGET_TS = 512
ATTN_TQ = 256
FOX_HEADS_PER_STEP = 2
SOFTMAX_ROWS = 16
OUT_TM = 512

_NT = (((1,), (1,)), ((), ()))


def _nt_dot(a, b):
    return lax.dot_general(a, b, _NT, preferred_element_type=jnp.float32)


def _params(*sem):
    return pltpu.CompilerParams(dimension_semantics=sem, vmem_limit_bytes=V7X_VMEM_LIMIT)


def _proj_kernel(x_ref, w_ref, o_ref, wb_ref, *, n_scaled_tiles, scale, w_transposed):
    j = pl.program_id(0)

    @pl.when(pl.program_id(1) == 0)
    def _():
        wb_ref[...] = w_ref[...].astype(wb_ref.dtype)

    if w_transposed:
        acc = _nt_dot(x_ref[...], wb_ref[...])
    else:
        acc = jnp.dot(x_ref[...], wb_ref[...], preferred_element_type=jnp.float32)
    s = jnp.where(j < n_scaled_tiles, jnp.float32(scale), jnp.float32(1.0))
    o_ref[...] = (acc * s).astype(o_ref.dtype)


def _project(xb2d, w, n, q_cols, scale, w_transposed):
    m, k = xb2d.shape
    tm, tn = PROJ_TM, PROJ_TN
    kern = functools.partial(_proj_kernel, n_scaled_tiles=q_cols // tn, scale=scale,
                             w_transposed=w_transposed)
    if w_transposed:
        w_spec = pl.BlockSpec((tn, k), lambda j, i: (j, 0))
        wb_shape = (tn, k)
    else:
        w_spec = pl.BlockSpec((k, tn), lambda j, i: (0, j))
        wb_shape = (k, tn)
    return pl.pallas_call(
        kern,
        grid=(n // tn, m // tm),
        in_specs=[pl.BlockSpec((tm, k), lambda j, i: (i, 0)), w_spec],
        out_specs=pl.BlockSpec((tm, tn), lambda j, i: (i, j)),
        out_shape=jax.ShapeDtypeStruct((m, n), jnp.bfloat16),
        scratch_shapes=[pltpu.VMEM(wb_shape, jnp.bfloat16)],
        compiler_params=_params("parallel", "arbitrary"),
        name="in_proj",
    )(xb2d, w)


def _split3(a):
    p1 = a.astype(jnp.bfloat16)
    r1 = a - p1.astype(jnp.float32)
    p2 = r1.astype(jnp.bfloat16)
    p3 = (r1 - p2.astype(jnp.float32)).astype(jnp.bfloat16)
    return p1, p2, p3


def _forget_kernel(x_ref, wt_ref, b_ref, c_ref, xb_ref, carry_ref):
    t = pl.program_id(1)
    ts = x_ref.shape[0]

    @pl.when(t == 0)
    def _():
        carry_ref[...] = jnp.zeros_like(carry_ref)

    xb = x_ref[...].astype(jnp.bfloat16)
    xb_ref[...] = xb
    f = _nt_dot(wt_ref[...].astype(jnp.bfloat16), xb)
    z = f + b_ref[...]
    logf = jnp.minimum(z, 0.0) - jnp.log1p(jnp.exp(-jnp.abs(z)))
    row = lax.broadcasted_iota(jnp.int32, (ts, ts), 0)
    col = lax.broadcasted_iota(jnp.int32, (ts, ts), 1)
    tri = (row <= col).astype(jnp.bfloat16)
    p1, p2, p3 = _split3(logf)
    csum = (jnp.dot(p1, tri, preferred_element_type=jnp.float32)
            + jnp.dot(p2, tri, preferred_element_type=jnp.float32)
            + jnp.dot(p3, tri, preferred_element_type=jnp.float32))
    c = csum + carry_ref[...]
    c_ref[...] = c * LOG2E
    carry_ref[...] = c[:, ts - 1:ts]


def _forget_cumsum(x, wf_t, b_f):
    bsz, s, d = x.shape
    h = wf_t.shape[0]
    ts = FORGET_TS
    return pl.pallas_call(
        _forget_kernel,
        grid=(bsz, s // ts),
        in_specs=[pl.BlockSpec((None, ts, d), lambda b, t: (b, t, 0)),
                  pl.BlockSpec((h, d), lambda b, t: (0, 0)),
                  pl.BlockSpec((h, 1), lambda b, t: (0, 0))],
        out_specs=[pl.BlockSpec((None, h, ts), lambda b, t: (b, 0, t)),
                   pl.BlockSpec((None, ts, d), lambda b, t: (b, t, 0))],
        out_shape=[jax.ShapeDtypeStruct((bsz, h, s), jnp.float32),
                   jax.ShapeDtypeStruct((bsz, s, d), jnp.bfloat16)],
        scratch_shapes=[pltpu.VMEM((h, 1), jnp.float32)],
        compiler_params=_params("parallel", "arbitrary"),
        name="forget_cumsum",
    )(x, wf_t, b_f.reshape(h, 1))


def _causal_tile_mask(t):
    row = lax.broadcasted_iota(jnp.int32, (t, t), 0)
    col = lax.broadcasted_iota(jnp.int32, (t, t), 1)
    return col <= row


def _row_reduce(parts, combine, lane_reduce):
    tiles = [s[:, j:j + LANES] for s in parts for j in range(0, s.shape[1], LANES)]
    return lane_reduce(functools.reduce(combine, tiles), axis=-1, keepdims=True)


def _exp2_softmax_rows(parts, dtype, with_sum):
    rows = parts[0].shape[0]
    out, sums = [], []
    for r in range(0, rows, SOFTMAX_ROWS):
        chunk = [s[r:r + SOFTMAX_ROWS] for s in parts]
        m = _row_reduce(chunk, jnp.maximum, jnp.max)
        ps = [jnp.exp2(s - m) for s in chunk]
        if with_sum:
            sums.append(_row_reduce(ps, jnp.add, jnp.sum))
        out.append(jnp.concatenate([p.astype(dtype) for p in ps], axis=1))
    p = jnp.concatenate(out, axis=0)
    return (p, jnp.concatenate(sums, axis=0)) if with_sum else p


def _pyramid_order(heads, n):
    items = [(h, i) for i in range(n) for h in range(heads)]
    return items[0::2] + items[1::2][::-1]


def _emit_pipelined(items, scores, softmax, output):
    n = len(items)
    s = {0: scores(items[0])}
    if n > 1:
        s[1] = scores(items[1])
    a = {0: softmax(items[0], s.pop(0))}
    for t in range(n):
        if t + 2 < n:
            s[t + 2] = scores(items[t + 2])
        if t + 1 < n:
            a[t + 1] = softmax(items[t + 1], s.pop(t + 1))
        output(items[t], a.pop(t))


def _fox_attn_kernel(q_ref, k_ref, v_ref, c_ref, o_ref, va_ref, *, tq):
    n = q_ref.shape[0] // tq
    d = HEAD_DIM
    heads = q_ref.shape[1] // d
    mask = _causal_tile_mask(tq)
    for h in range(heads):
        va_ref[h, :, :d] = v_ref[:, h * d:(h + 1) * d]
        va_ref[h, :, d:] = jnp.ones((va_ref.shape[1], d), va_ref.dtype)

    def scores(item):
        h, i = item
        lo = i * tq
        cols = slice(h * d, (h + 1) * d)
        s = _nt_dot(q_ref[lo:lo + tq, cols], k_ref[:lo + tq, cols]) - c_ref[h, :, :lo + tq]
        diag = jnp.where(mask, s[:, lo:], MASK_VALUE)
        return [s[:, :lo], diag] if i > 0 else [diag]

    def softmax(item, parts):
        return _exp2_softmax_rows(parts, va_ref.dtype, with_sum=False)

    def output(item, p):
        h, i = item
        hi = (i + 1) * tq
        pv = jnp.dot(p, va_ref[h, :hi, :], preferred_element_type=jnp.float32)
        o_ref[hi - tq:hi, h * d:(h + 1) * d] = (pv[:, :d] / pv[:, d:]).astype(o_ref.dtype)

    _emit_pipelined(_pyramid_order(heads, n), scores, softmax, output)


def _fox_attention(proj, c, bsz, s):
    h, d, hs = FOX_HEADS, HEAD_DIM, FOX_HEADS_PER_STEP
    steps = h // hs
    w = hs * d
    kern = functools.partial(_fox_attn_kernel, tq=ATTN_TQ)
    return pl.pallas_call(
        kern,
        grid=(bsz, steps),
        in_specs=[pl.BlockSpec((None, s, w), lambda b, g: (b, 0, g)),
                  pl.BlockSpec((None, s, w), lambda b, g: (b, 0, steps + g)),
                  pl.BlockSpec((None, s, w), lambda b, g: (b, 0, 2 * steps + g)),
                  pl.BlockSpec((None, hs, 1, s), lambda b, g: (b, g, 0, 0))],
        out_specs=pl.BlockSpec((None, s, w), lambda b, g: (b, 0, g)),
        out_shape=jax.ShapeDtypeStruct((bsz, s, h * d), jnp.bfloat16),
        scratch_shapes=[pltpu.VMEM((hs, s, 2 * d), jnp.bfloat16)],
        compiler_params=_params("parallel", "parallel"),
        name="fox_attn",
    )(proj, proj, proj, c)


def _bias_kernel(tab_ref, o_ref):
    h = pl.program_id(0)
    r = pl.program_id(1)
    t = o_ref.shape[0]
    dq = lax.broadcasted_iota(jnp.int32, (t, t), 0)
    dk = lax.broadcasted_iota(jnp.int32, (t, t), 1)
    rel = r * t + dq - dk
    n = jnp.maximum(rel, 0)
    max_exact = NUM_BUCKETS // 2
    large = max_exact + (jnp.log(jnp.maximum(n, 1).astype(jnp.float32) / max_exact)
                         / math.log(MAX_DISTANCE / max_exact)
                         * (NUM_BUCKETS - max_exact)).astype(jnp.int32)
    large = jnp.minimum(large, NUM_BUCKETS - 1)
    bucket = jnp.where(n < max_exact, n, large)
    val = jnp.zeros((t, t), jnp.float32)
    for b in range(NUM_BUCKETS):
        val = jnp.where(bucket == b, tab_ref[b, h], val)
    val = (val - tab_ref[NUM_BUCKETS - 1, h]) * LOG2E
    o_ref[...] = jnp.where(rel >= 0, val, MASK_VALUE)


def _bias_tiles(rel_bias, t):
    nh = rel_bias.shape[1]
    return pl.pallas_call(
        _bias_kernel,
        grid=(nh, 2),
        in_specs=[pl.BlockSpec(memory_space=pltpu.SMEM)],
        out_specs=pl.BlockSpec((None, None, t, t), lambda h, r: (h, r, 0, 0)),
        out_shape=jax.ShapeDtypeStruct((nh, 2, t, t), jnp.float32),
        compiler_params=_params("parallel", "parallel"),
        name="bias_tiles",
    )(rel_bias)


def _diff_attn_kernel(q_ref, k_ref, v_ref, bias_ref, lq_ref, lk_ref, g_ref, o_ref,
                      *, tq, lambda_init):
    n = q_ref.shape[0] // tq
    d = HEAD_DIM
    lq = lq_ref[...]
    lk = lk_ref[...]
    dots = jnp.sum(lq * lk, axis=-1, keepdims=True)
    lam = jnp.exp(dots[0:1]) - jnp.exp(dots[1:2]) + lambda_init

    def scores(i):
        lo = i * tq
        maps = []
        for idx in range(2):
            cols = slice(idx * d, (idx + 1) * d)
            s = _nt_dot(q_ref[lo:lo + tq, cols], k_ref[:lo + tq, cols])
            parts = [s[:, :lo - tq]] if i > 1 else []
            if i > 0:
                parts.append(s[:, lo - tq:lo] + bias_ref[1])
            parts.append(s[:, lo:] + bias_ref[0])
            maps.append(parts)
        return maps

    def softmax(i, maps):
        p1, l1 = _exp2_softmax_rows(maps[0], v_ref.dtype, with_sum=True)
        p2, l2 = _exp2_softmax_rows(maps[1], v_ref.dtype, with_sum=True)
        return jnp.concatenate([p1, p2], axis=0), 1.0 / l1, lam / l2

    def output(i, args):
        p, r1, r2 = args
        hi = (i + 1) * tq
        pv = jnp.dot(p, v_ref[:hi, :], preferred_element_type=jnp.float32)
        o = pv[:tq] * r1 - pv[tq:] * r2
        o = o * lax.rsqrt(jnp.mean(o * o, axis=-1, keepdims=True) + RMS_EPS) * g_ref[...]
        o_ref[hi - tq:hi, :] = (o * (1.0 - lambda_init)).astype(o_ref.dtype)

    _emit_pipelined([i for _, i in _pyramid_order(1, n)], scores, softmax, output)


def _diff_attention(proj, bias, lam_q, lam_k, subln_g, bsz, s, lambda_init):
    t = ATTN_TQ
    h, d2 = DIFF_HEADS, 2 * HEAD_DIM
    kern = functools.partial(_diff_attn_kernel, tq=t, lambda_init=lambda_init)
    return pl.pallas_call(
        kern,
        grid=(bsz, h),
        in_specs=[pl.BlockSpec((None, s, d2), lambda b, hh: (b, 0, hh)),
                  pl.BlockSpec((None, s, d2), lambda b, hh: (b, 0, h + hh)),
                  pl.BlockSpec((None, s, d2), lambda b, hh: (b, 0, 2 * h + hh)),
                  pl.BlockSpec((None, 2, t, t), lambda b, hh: (hh, 0, 0, 0)),
                  pl.BlockSpec((2, HEAD_DIM), lambda b, hh: (0, 0)),
                  pl.BlockSpec((2, HEAD_DIM), lambda b, hh: (0, 0)),
                  pl.BlockSpec((1, d2), lambda b, hh: (0, 0))],
        out_specs=pl.BlockSpec((None, s, d2), lambda b, hh: (b, 0, hh)),
        out_shape=jax.ShapeDtypeStruct((bsz, s, h * d2), jnp.bfloat16),
        compiler_params=_params("parallel", "parallel"),
        name="diff_attn",
    )(proj, proj, proj, bias, lam_q, lam_k, subln_g.reshape(1, d2))


def _out_ln_kernel(o_ref, z_ref, x_ref, w_ref, g_ref, b_ref, y_ref, *maybe_yb_ref, alpha):
    o = o_ref[...].astype(jnp.float32)
    z = z_ref[...].astype(jnp.float32)
    gated = o * (z / (1.0 + jnp.exp(-z)))
    acc = jnp.dot(gated.astype(w_ref.dtype), w_ref[...], preferred_element_type=jnp.float32)
    v = alpha * x_ref[...] + acc
    mu = jnp.mean(v, axis=-1, keepdims=True)
    vc = v - mu
    var = jnp.mean(vc * vc, axis=-1, keepdims=True)
    y = vc * lax.rsqrt(var + LN_EPS) * g_ref[...] + b_ref[...]
    y_ref[...] = y
    for yb_ref in maybe_yb_ref:
        yb_ref[...] = y.astype(yb_ref.dtype)


def _out_ln(o2d, proj2d, x2d, w_bf16, g, b, alpha, emit_bf16):
    m, d = x2d.shape
    br = o2d.shape[1]
    tm = OUT_TM
    z_block = proj2d.shape[1] // br - 1
    kern = functools.partial(_out_ln_kernel, alpha=alpha)
    resident = dict(pipeline_mode=pl.Buffered(1))
    out_specs = [pl.BlockSpec((tm, d), lambda i: (i, 0))]
    out_shape = [jax.ShapeDtypeStruct((m, d), jnp.float32)]
    if emit_bf16:
        out_specs.append(pl.BlockSpec((tm, d), lambda i: (i, 0)))
        out_shape.append(jax.ShapeDtypeStruct((m, d), jnp.bfloat16))
    return pl.pallas_call(
        kern,
        grid=(m // tm,),
        in_specs=[pl.BlockSpec((tm, br), lambda i: (i, 0)),
                  pl.BlockSpec((tm, br), lambda i: (i, z_block)),
                  pl.BlockSpec((tm, d), lambda i: (i, 0)),
                  pl.BlockSpec((br, d), lambda i: (0, 0), **resident),
                  pl.BlockSpec((1, d), lambda i: (0, 0), **resident),
                  pl.BlockSpec((1, d), lambda i: (0, 0), **resident)],
        out_specs=out_specs,
        out_shape=out_shape,
        compiler_params=_params("parallel"),
        name="out_ln",
    )(o2d, proj2d, x2d, w_bf16, g.reshape(1, d), b.reshape(1, d))


def kernel(x, fox_w_in, fox_b_f, fox_w_out, diff_w_in, diff_lam_q, diff_lam_k,
           diff_subln_g, diff_w_out, rel_bias, ln_g, ln_b):
    bsz, s, d = x.shape
    depth = ln_g.shape[0]
    branch = fox_w_out.shape[1]
    alpha = (2 * depth) ** 0.25
    scale = HEAD_DIM ** -0.5 * LOG2E
    m = bsz * s
    bf16 = jnp.bfloat16

    x2d = x.reshape(m, d)
    xb2d = None
    for i in range(depth):
        j = i // 2
        if i % 2 == 0:
            w_in_t = fox_w_in[j].T
            c, xb = _forget_cumsum(x2d.reshape(bsz, s, d), w_in_t[4 * branch:], fox_b_f[j])
            xb2d = xb.reshape(m, d)
            proj = _project(xb2d, w_in_t, 4 * branch, branch, scale, w_transposed=True)
            o = _fox_attention(proj.reshape(bsz, s, 4 * branch),
                               c.reshape(bsz, FOX_HEADS, 1, s), bsz, s)
            w_out = fox_w_out[j]
        else:
            lambda_init = 0.8 - 0.6 * math.exp(-0.3 * i)
            proj = _project(xb2d, diff_w_in[j], 4 * branch, branch, scale, w_transposed=False)
            bias = _bias_tiles(rel_bias, ATTN_TQ)
            o = _diff_attention(proj.reshape(bsz, s, 4 * branch), bias, diff_lam_q[j],
                                diff_lam_k[j], diff_subln_g[j], bsz, s, lambda_init)
            w_out = diff_w_out[j]
        next_is_diff = i + 1 < depth and (i + 1) % 2 == 1
        outs = _out_ln(o.reshape(m, branch), proj, x2d, w_out.astype(bf16),
                       ln_g[i], ln_b[i], alpha, emit_bf16=next_is_diff)
        x2d = outs[0]
        xb2d = outs[1] if next_is_diff else None
    return x2d.reshape(bsz, s, d)
```

```python
import functools
import math

import jax
import jax.numpy as jnp
from jax import lax
from jax.experimental import pallas as pl
from jax.experimental.pallas import tpu as pltpu

LANES = 128
HEAD_DIM = 128
FOX_HEADS = 16
DIFF_HEADS = 8
NUM_BUCKETS = 32
MAX_DISTANCE = 128
LN_EPS = 1e-5
RMS_EPS = 1e-5
MASK_VALUE = -1e30
LOG2E = math.log2(math.e)

V7X_VMEM_LIMIT = 56 * 1024 * 1024

PROJ_TM = 2048
PROJ_TN = 1024
FORGET_TS = 512
ATTN_TQ = 256
FOX_HEADS_PER_STEP = 2
OUT_TM = 512

_NT = (((1,), (1,)), ((), ()))


def _nt_dot(a, b):
    return lax.dot_general(a, b, _NT, preferred_element_type=jnp.float32)


def _params(*sem):
    return pltpu.CompilerParams(dimension_semantics=sem, vmem_limit_bytes=V7X_VMEM_LIMIT)


def _proj_kernel(x_ref, w_ref, wo_ref, o_ref, wob_ref, wb_ref, *, n_scaled_tiles, scale,
                 w_transposed):
    j = pl.program_id(0)
    wob_ref[...] = wo_ref[...].astype(wob_ref.dtype)

    @pl.when(pl.program_id(1) == 0)
    def _():
        wb_ref[...] = w_ref[...].astype(wb_ref.dtype)

    if w_transposed:
        acc = _nt_dot(x_ref[...], wb_ref[...])
    else:
        acc = jnp.dot(x_ref[...], wb_ref[...], preferred_element_type=jnp.float32)
    s = jnp.where(j < n_scaled_tiles, jnp.float32(scale), jnp.float32(1.0))
    o_ref[...] = (acc * s).astype(o_ref.dtype)


def _project(xb2d, w, n, q_cols, scale, w_transposed, w_out):
    m, k = xb2d.shape
    tm, tn = PROJ_TM, PROJ_TN
    n_i = m // tm
    wo_rows = w_out.shape[0] // ((n // tn) * n_i)
    kern = functools.partial(_proj_kernel, n_scaled_tiles=q_cols // tn, scale=scale,
                             w_transposed=w_transposed)
    if w_transposed:
        w_spec = pl.BlockSpec((tn, k), lambda j, i: (j, 0))
        wb_shape = (tn, k)
    else:
        w_spec = pl.BlockSpec((k, tn), lambda j, i: (0, j))
        wb_shape = (k, tn)
    wo_spec = pl.BlockSpec((wo_rows, w_out.shape[1]), lambda j, i: (j * n_i + i, 0))
    return pl.pallas_call(
        kern,
        grid=(n // tn, n_i),
        in_specs=[pl.BlockSpec((tm, k), lambda j, i: (i, 0)), w_spec, wo_spec],
        out_specs=[pl.BlockSpec((tm, tn), lambda j, i: (i, j)), wo_spec],
        out_shape=[jax.ShapeDtypeStruct((m, n), jnp.bfloat16),
                   jax.ShapeDtypeStruct(w_out.shape, jnp.bfloat16)],
        scratch_shapes=[pltpu.VMEM(wb_shape, jnp.bfloat16)],
        compiler_params=_params("parallel", "arbitrary"),
        name="in_proj",
    )(xb2d, w, w_out)


def _split3(a):
    p1 = a.astype(jnp.bfloat16)
    r1 = a - p1.astype(jnp.float32)
    p2 = r1.astype(jnp.bfloat16)
    p3 = (r1 - p2.astype(jnp.float32)).astype(jnp.bfloat16)
    return p1, p2, p3


def _forget_kernel(x_ref, wt_ref, b_ref, c_ref, xb_ref, carry_ref):
    t = pl.program_id(1)
    ts = x_ref.shape[0]

    @pl.when(t == 0)
    def _():
        carry_ref[...] = jnp.zeros_like(carry_ref)

    xb = x_ref[...].astype(jnp.bfloat16)
    xb_ref[...] = xb
    f = _nt_dot(wt_ref[...].astype(jnp.bfloat16), xb)
    z = f + b_ref[...]
    logf = jnp.minimum(z, 0.0) - jnp.log1p(jnp.exp(-jnp.abs(z)))
    row = lax.broadcasted_iota(jnp.int32, (ts, ts), 0)
    col = lax.broadcasted_iota(jnp.int32, (ts, ts), 1)
    tri = (row <= col).astype(jnp.bfloat16)
    p1, p2, p3 = _split3(logf)
    csum = (jnp.dot(p1, tri, preferred_element_type=jnp.float32)
            + jnp.dot(p2, tri, preferred_element_type=jnp.float32)
            + jnp.dot(p3, tri, preferred_element_type=jnp.float32))
    c = csum + carry_ref[...]
    c_ref[...] = c * LOG2E
    carry_ref[...] = c[:, ts - 1:ts]


def _forget_cumsum(x, wf_t, b_f):
    bsz, s, d = x.shape
    h = wf_t.shape[0]
    ts = FORGET_TS
    return pl.pallas_call(
        _forget_kernel,
        grid=(bsz, s // ts),
        in_specs=[pl.BlockSpec((None, ts, d), lambda b, t: (b, t, 0)),
                  pl.BlockSpec((h, d), lambda b, t: (0, 0)),
                  pl.BlockSpec((h, 1), lambda b, t: (0, 0))],
        out_specs=[pl.BlockSpec((None, h, ts), lambda b, t: (b, 0, t)),
                   pl.BlockSpec((None, ts, d), lambda b, t: (b, t, 0))],
        out_shape=[jax.ShapeDtypeStruct((bsz, h, s), jnp.float32),
                   jax.ShapeDtypeStruct((bsz, s, d), jnp.bfloat16)],
        scratch_shapes=[pltpu.VMEM((h, 1), jnp.float32)],
        compiler_params=_params("parallel", "arbitrary"),
        name="forget_cumsum",
    )(x, wf_t, b_f.reshape(h, 1))


def _causal_tile_mask(t):
    row = lax.broadcasted_iota(jnp.int32, (t, t), 0)
    col = lax.broadcasted_iota(jnp.int32, (t, t), 1)
    return col <= row


def _row_reduce(parts, combine, lane_reduce):
    tiles = [s[:, j:j + LANES] for s in parts for j in range(0, s.shape[1], LANES)]
    return lane_reduce(functools.reduce(combine, tiles), axis=-1, keepdims=True)


def _exp2_softmax_rows(parts, dtype, with_sum):
    m = _row_reduce(parts, jnp.maximum, jnp.max)
    ps = [jnp.exp2(s - m) for s in parts]
    p = jnp.concatenate([x.astype(dtype) for x in ps], axis=1)
    return (p, _row_reduce(ps, jnp.add, jnp.sum)) if with_sum else p


def _pyramid_order(heads, n):
    items = [(h, i) for i in range(n) for h in range(heads)]
    return items[0::2] + items[1::2][::-1]


def _emit_pipelined(items, scores, softmax, output):
    n = len(items)
    s = {0: scores(items[0])}
    if n > 1:
        s[1] = scores(items[1])
    a = {0: softmax(items[0], s.pop(0))}
    for t in range(n):
        if t + 2 < n:
            s[t + 2] = scores(items[t + 2])
        if t + 1 < n:
            a[t + 1] = softmax(items[t + 1], s.pop(t + 1))
        output(items[t], a.pop(t))


def _fox_attn_kernel(q_ref, k_ref, v_ref, c_ref, o_ref, va_ref, *, tq):
    n = q_ref.shape[0] // tq
    d = HEAD_DIM
    heads = q_ref.shape[1] // d
    mask = _causal_tile_mask(tq)
    for h in range(heads):
        va_ref[h, :, :d] = v_ref[:, h * d:(h + 1) * d]
        va_ref[h, :, d:] = jnp.ones((va_ref.shape[1], d), va_ref.dtype)

    def scores(item):
        h, i = item
        lo = i * tq
        cols = slice(h * d, (h + 1) * d)
        s = _nt_dot(q_ref[lo:lo + tq, cols], k_ref[:lo + tq, cols]) - c_ref[h, :, :lo + tq]
        diag = jnp.where(mask, s[:, lo:], MASK_VALUE)
        return [s[:, :lo], diag] if i > 0 else [diag]

    def softmax(item, parts):
        return _exp2_softmax_rows(parts, va_ref.dtype, with_sum=False)

    def output(item, p):
        h, i = item
        hi = (i + 1) * tq
        pv = jnp.dot(p, va_ref[h, :hi, :], preferred_element_type=jnp.float32)
        o_ref[hi - tq:hi, h * d:(h + 1) * d] = (pv[:, :d] / pv[:, d:]).astype(o_ref.dtype)

    _emit_pipelined(_pyramid_order(heads, n), scores, softmax, output)


def _fox_attention(proj, c, bsz, s):
    h, d, hs = FOX_HEADS, HEAD_DIM, FOX_HEADS_PER_STEP
    steps = h // hs
    w = hs * d
    kern = functools.partial(_fox_attn_kernel, tq=ATTN_TQ)
    return pl.pallas_call(
        kern,
        grid=(bsz, steps),
        in_specs=[pl.BlockSpec((None, s, w), lambda b, g: (b, 0, g)),
                  pl.BlockSpec((None, s, w), lambda b, g: (b, 0, steps + g)),
                  pl.BlockSpec((None, s, w), lambda b, g: (b, 0, 2 * steps + g)),
                  pl.BlockSpec((None, hs, 1, s), lambda b, g: (b, g, 0, 0))],
        out_specs=pl.BlockSpec((None, s, w), lambda b, g: (b, 0, g)),
        out_shape=jax.ShapeDtypeStruct((bsz, s, h * d), jnp.bfloat16),
        scratch_shapes=[pltpu.VMEM((hs, s, 2 * d), jnp.bfloat16)],
        compiler_params=_params("parallel", "parallel"),
        name="fox_attn",
    )(proj, proj, proj, c)


def _bias_kernel(tab_ref, o_ref):
    h = pl.program_id(0)
    r = pl.program_id(1)
    t = o_ref.shape[0]
    dq = lax.broadcasted_iota(jnp.int32, (t, t), 0)
    dk = lax.broadcasted_iota(jnp.int32, (t, t), 1)
    rel = r * t + dq - dk
    n = jnp.maximum(rel, 0)
    max_exact = NUM_BUCKETS // 2
    large = max_exact + (jnp.log(jnp.maximum(n, 1).astype(jnp.float32) / max_exact)
                         / math.log(MAX_DISTANCE / max_exact)
                         * (NUM_BUCKETS - max_exact)).astype(jnp.int32)
    large = jnp.minimum(large, NUM_BUCKETS - 1)
    bucket = jnp.where(n < max_exact, n, large)
    val = jnp.zeros((t, t), jnp.float32)
    for b in range(NUM_BUCKETS):
        val = jnp.where(bucket == b, tab_ref[b, h], val)
    val = (val - tab_ref[NUM_BUCKETS - 1, h]) * LOG2E
    o_ref[...] = jnp.where(rel >= 0, val, MASK_VALUE)


def _bias_tiles(rel_bias, t):
    nh = rel_bias.shape[1]
    return pl.pallas_call(
        _bias_kernel,
        grid=(nh, 2),
        in_specs=[pl.BlockSpec(memory_space=pltpu.SMEM)],
        out_specs=pl.BlockSpec((None, None, t, t), lambda h, r: (h, r, 0, 0)),
        out_shape=jax.ShapeDtypeStruct((nh, 2, t, t), jnp.float32),
        compiler_params=_params("parallel", "parallel"),
        name="bias_tiles",
    )(rel_bias)


def _diff_attn_kernel(q_ref, k_ref, v_ref, bias_ref, lq_ref, lk_ref, g_ref, o_ref,
                      *, tq, lambda_init):
    n = q_ref.shape[0] // tq
    d = HEAD_DIM
    lq = lq_ref[...]
    lk = lk_ref[...]
    dots = jnp.sum(lq * lk, axis=-1, keepdims=True)
    lam = jnp.exp(dots[0:1]) - jnp.exp(dots[1:2]) + lambda_init

    def scores(i):
        lo = i * tq
        maps = []
        for idx in range(2):
            cols = slice(idx * d, (idx + 1) * d)
            s = _nt_dot(q_ref[lo:lo + tq, cols], k_ref[:lo + tq, cols])
            parts = [s[:, :lo - tq]] if i > 1 else []
            if i > 0:
                parts.append(s[:, lo - tq:lo] + bias_ref[1])
            parts.append(s[:, lo:] + bias_ref[0])
            maps.append(parts)
        return maps

    def softmax(i, maps):
        p1, l1 = _exp2_softmax_rows(maps[0], v_ref.dtype, with_sum=True)
        p2, l2 = _exp2_softmax_rows(maps[1], v_ref.dtype, with_sum=True)
        return jnp.concatenate([p1, p2], axis=0), 1.0 / l1, lam / l2

    def output(i, args):
        p, r1, r2 = args
        hi = (i + 1) * tq
        pv = jnp.dot(p, v_ref[:hi, :], preferred_element_type=jnp.float32)
        o = pv[:tq] * r1 - pv[tq:] * r2
        o = o * lax.rsqrt(jnp.mean(o * o, axis=-1, keepdims=True) + RMS_EPS) * g_ref[...]
        o_ref[hi - tq:hi, :] = (o * (1.0 - lambda_init)).astype(o_ref.dtype)

    _emit_pipelined(list(range(n)), scores, softmax, output)


def _diff_attention(proj, bias, lam_q, lam_k, subln_g, bsz, s, lambda_init):
    t = ATTN_TQ
    h, d2 = DIFF_HEADS, 2 * HEAD_DIM
    kern = functools.partial(_diff_attn_kernel, tq=t, lambda_init=lambda_init)
    return pl.pallas_call(
        kern,
        grid=(bsz, h),
        in_specs=[pl.BlockSpec((None, s, d2), lambda b, hh: (b, 0, hh)),
                  pl.BlockSpec((None, s, d2), lambda b, hh: (b, 0, h + hh)),
                  pl.BlockSpec((None, s, d2), lambda b, hh: (b, 0, 2 * h + hh)),
                  pl.BlockSpec((None, 2, t, t), lambda b, hh: (hh, 0, 0, 0)),
                  pl.BlockSpec((2, HEAD_DIM), lambda b, hh: (0, 0)),
                  pl.BlockSpec((2, HEAD_DIM), lambda b, hh: (0, 0)),
                  pl.BlockSpec((1, d2), lambda b, hh: (0, 0))],
        out_specs=pl.BlockSpec((None, s, d2), lambda b, hh: (b, 0, hh)),
        out_shape=jax.ShapeDtypeStruct((bsz, s, h * d2), jnp.bfloat16),
        compiler_params=_params("parallel", "parallel"),
        name="diff_attn",
    )(proj, proj, proj, bias, lam_q, lam_k, subln_g.reshape(1, d2))


def _out_ln_kernel(o_ref, z_ref, x_ref, w_ref, g_ref, b_ref, y_ref, *maybe_yb_ref, alpha):
    o = o_ref[...].astype(jnp.float32)
    z = z_ref[...].astype(jnp.float32)
    gated = o * (z / (1.0 + jnp.exp(-z)))
    acc = jnp.dot(gated.astype(w_ref.dtype), w_ref[...], preferred_element_type=jnp.float32)
    v = alpha * x_ref[...] + acc
    mu = jnp.mean(v, axis=-1, keepdims=True)
    vc = v - mu
    var = jnp.mean(vc * vc, axis=-1, keepdims=True)
    y = vc * lax.rsqrt(var + LN_EPS) * g_ref[...] + b_ref[...]
    y_ref[...] = y
    for yb_ref in maybe_yb_ref:
        yb_ref[...] = y.astype(yb_ref.dtype)


def _out_ln(o2d, proj2d, x2d, w_bf16, g, b, alpha, emit_bf16):
    m, d = x2d.shape
    br = o2d.shape[1]
    tm = OUT_TM
    z_block = proj2d.shape[1] // br - 1
    kern = functools.partial(_out_ln_kernel, alpha=alpha)
    resident = dict(pipeline_mode=pl.Buffered(1))
    out_specs = [pl.BlockSpec((tm, d), lambda i: (i, 0))]
    out_shape = [jax.ShapeDtypeStruct((m, d), jnp.float32)]
    if emit_bf16:
        out_specs.append(pl.BlockSpec((tm, d), lambda i: (i, 0)))
        out_shape.append(jax.ShapeDtypeStruct((m, d), jnp.bfloat16))
    return pl.pallas_call(
        kern,
        grid=(m // tm,),
        in_specs=[pl.BlockSpec((tm, br), lambda i: (i, 0)),
                  pl.BlockSpec((tm, br), lambda i: (i, z_block)),
                  pl.BlockSpec((tm, d), lambda i: (i, 0)),
                  pl.BlockSpec((br, d), lambda i: (0, 0), **resident),
                  pl.BlockSpec((1, d), lambda i: (0, 0), **resident),
                  pl.BlockSpec((1, d), lambda i: (0, 0), **resident)],
        out_specs=out_specs,
        out_shape=out_shape,
        compiler_params=_params("parallel"),
        name="out_ln",
    )(o2d, proj2d, x2d, w_bf16, g.reshape(1, d), b.reshape(1, d))


def kernel(x, fox_w_in, fox_b_f, fox_w_out, diff_w_in, diff_lam_q, diff_lam_k,
           diff_subln_g, diff_w_out, rel_bias, ln_g, ln_b):
    bsz, s, d = x.shape
    depth = ln_g.shape[0]
    branch = fox_w_out.shape[1]
    alpha = (2 * depth) ** 0.25
    scale = HEAD_DIM ** -0.5 * LOG2E
    m = bsz * s

    x2d = x.reshape(m, d)
    xb2d = None
    for i in range(depth):
        j = i // 2
        if i % 2 == 0:
            w_in_t = fox_w_in[j].T
            c, xb = _forget_cumsum(x2d.reshape(bsz, s, d), w_in_t[4 * branch:], fox_b_f[j])
            xb2d = xb.reshape(m, d)
            proj, w_out_b = _project(xb2d, w_in_t, 4 * branch, branch, scale,
                                     w_transposed=True, w_out=fox_w_out[j])
            o = _fox_attention(proj.reshape(bsz, s, 4 * branch),
                               c.reshape(bsz, FOX_HEADS, 1, s), bsz, s)
        else:
            lambda_init = 0.8 - 0.6 * math.exp(-0.3 * i)
            proj, w_out_b = _project(xb2d, diff_w_in[j], 4 * branch, branch, scale,
                                     w_transposed=False, w_out=diff_w_out[j])
            bias = _bias_tiles(rel_bias, ATTN_TQ)
            o = _diff_attention(proj.reshape(bsz, s, 4 * branch), bias, diff_lam_q[j],
                                diff_lam_k[j], diff_subln_g[j], bsz, s, lambda_init)
        next_is_diff = i + 1 < depth and (i + 1) % 2 == 1
        outs = _out_ln(o.reshape(m, branch), proj, x2d, w_out_b,
                       ln_g[i], ln_b[i], alpha, emit_bf16=next_is_diff)
        x2d = outs[0]
        xb2d = outs[1] if next_is_diff else None
    return x2d.reshape(bsz, s, d)
```

```python
import functools
import math

import jax
import jax.numpy as jnp
from jax import lax
from jax.experimental import pallas as pl
from jax.experimental.pallas import tpu as pltpu

LANES = 128
HEAD_DIM = 128
FOX_HEADS = 16
DIFF_HEADS = 8
NUM_BUCKETS = 32
MAX_DISTANCE = 128
LN_EPS = 1e-5
RMS_EPS = 1e-5
MASK_VALUE = -1e30
LOG2E = math.log2(math.e)

V7X_VMEM_LIMIT = 56 * 1024 * 1024

PROJ_TM = 2048
PROJ_TN = 1024
FORGET_TS = 1024
ATTN_TQ = 256
FOX_HEADS_PER_STEP = 4
OUT_TM = 512

_NT = (((1,), (1,)), ((), ()))


def _nt_dot(a, b):
    return lax.dot_general(a, b, _NT, preferred_element_type=jnp.float32)


def _params(*sem):
    return pltpu.CompilerParams(dimension_semantics=sem, vmem_limit_bytes=V7X_VMEM_LIMIT)


def _proj_kernel(x_ref, w_ref, wo_ref, o_ref, wob_ref, wb_ref, *, n_scaled_tiles, scale,
                 w_transposed):
    j = pl.program_id(0)
    wob_ref[...] = wo_ref[...].astype(wob_ref.dtype)

    @pl.when(pl.program_id(1) == 0)
    def _():
        wb_ref[...] = w_ref[...].astype(wb_ref.dtype)

    if w_transposed:
        acc = _nt_dot(x_ref[...], wb_ref[...])
    else:
        acc = jnp.dot(x_ref[...], wb_ref[...], preferred_element_type=jnp.float32)
    s = jnp.where(j < n_scaled_tiles, jnp.float32(scale), jnp.float32(1.0))
    o_ref[...] = (acc * s).astype(o_ref.dtype)


def _project(xb2d, w, n, q_cols, scale, w_transposed, w_out):
    m, k = xb2d.shape
    tm, tn = PROJ_TM, PROJ_TN
    n_i = m // tm
    wo_rows = w_out.shape[0] // ((n // tn) * n_i)
    kern = functools.partial(_proj_kernel, n_scaled_tiles=q_cols // tn, scale=scale,
                             w_transposed=w_transposed)
    if w_transposed:
        w_spec = pl.BlockSpec((tn, k), lambda j, i: (j, 0))
        wb_shape = (tn, k)
    else:
        w_spec = pl.BlockSpec((k, tn), lambda j, i: (0, j))
        wb_shape = (k, tn)
    wo_spec = pl.BlockSpec((wo_rows, w_out.shape[1]), lambda j, i: (j * n_i + i, 0))
    return pl.pallas_call(
        kern,
        grid=(n // tn, n_i),
        in_specs=[pl.BlockSpec((tm, k), lambda j, i: (i, 0)), w_spec, wo_spec],
        out_specs=[pl.BlockSpec((tm, tn), lambda j, i: (i, j)), wo_spec],
        out_shape=[jax.ShapeDtypeStruct((m, n), jnp.bfloat16),
                   jax.ShapeDtypeStruct(w_out.shape, jnp.bfloat16)],
        scratch_shapes=[pltpu.VMEM(wb_shape, jnp.bfloat16)],
        compiler_params=_params("parallel", "arbitrary"),
        name="in_proj",
    )(xb2d, w, w_out)


def _split3(a):
    p1 = a.astype(jnp.bfloat16)
    r1 = a - p1.astype(jnp.float32)
    p2 = r1.astype(jnp.bfloat16)
    p3 = (r1 - p2.astype(jnp.float32)).astype(jnp.bfloat16)
    return p1, p2, p3


def _forget_kernel(x_ref, wt_ref, b_ref, c_ref, xb_ref, carry_ref):
    t = pl.program_id(1)
    ts = x_ref.shape[0]

    @pl.when(t == 0)
    def _():
        carry_ref[...] = jnp.zeros_like(carry_ref)

    xb = x_ref[...].astype(jnp.bfloat16)
    xb_ref[...] = xb
    f = _nt_dot(wt_ref[...].astype(jnp.bfloat16), xb)
    z = f + b_ref[...]
    logf = jnp.minimum(z, 0.0) - jnp.log1p(jnp.exp(-jnp.abs(z)))
    row = lax.broadcasted_iota(jnp.int32, (ts, ts), 0)
    col = lax.broadcasted_iota(jnp.int32, (ts, ts), 1)
    tri = (row <= col).astype(jnp.bfloat16)
    h = logf.shape[0]
    sums = jnp.dot(jnp.concatenate(_split3(logf), axis=0), tri,
                   preferred_element_type=jnp.float32)
    c = (sums[:h] + sums[h:2 * h]) + sums[2 * h:] + carry_ref[...]
    c_ref[...] = c * LOG2E
    carry_ref[...] = c[:, ts - 1:ts]


def _forget_cumsum(x, wf_t, b_f):
    bsz, s, d = x.shape
    h = wf_t.shape[0]
    ts = FORGET_TS
    return pl.pallas_call(
        _forget_kernel,
        grid=(bsz, s // ts),
        in_specs=[pl.BlockSpec((None, ts, d), lambda b, t: (b, t, 0)),
                  pl.BlockSpec((h, d), lambda b, t: (0, 0)),
                  pl.BlockSpec((h, 1), lambda b, t: (0, 0))],
        out_specs=[pl.BlockSpec((None, h, ts), lambda b, t: (b, 0, t)),
                   pl.BlockSpec((None, ts, d), lambda b, t: (b, t, 0))],
        out_shape=[jax.ShapeDtypeStruct((bsz, h, s), jnp.float32),
                   jax.ShapeDtypeStruct((bsz, s, d), jnp.bfloat16)],
        scratch_shapes=[pltpu.VMEM((h, 1), jnp.float32)],
        compiler_params=_params("parallel", "arbitrary"),
        name="forget_cumsum",
    )(x, wf_t, b_f.reshape(h, 1))


def _causal_tile_mask(t):
    row = lax.broadcasted_iota(jnp.int32, (t, t), 0)
    col = lax.broadcasted_iota(jnp.int32, (t, t), 1)
    return col <= row


def _row_reduce(parts, combine, lane_reduce):
    tiles = [s[:, j:j + LANES] for s in parts for j in range(0, s.shape[1], LANES)]
    return lane_reduce(functools.reduce(combine, tiles), axis=-1, keepdims=True)


def _exp2_softmax_rows(parts, dtype, with_sum):
    m = _row_reduce(parts, jnp.maximum, jnp.max)
    ps = [jnp.exp2(s - m) for s in parts]
    p = jnp.concatenate([x.astype(dtype) for x in ps], axis=1)
    return (p, _row_reduce(ps, jnp.add, jnp.sum)) if with_sum else p


def _pyramid_order(heads, n):
    items = [(h, i) for i in range(n) for h in range(heads)]
    return items[0::2] + items[1::2][::-1]


def _emit_pipelined(items, scores, softmax, output):
    n = len(items)
    s = {0: scores(items[0])}
    if n > 1:
        s[1] = scores(items[1])
    a = {0: softmax(items[0], s.pop(0))}
    for t in range(n):
        if t + 2 < n:
            s[t + 2] = scores(items[t + 2])
        if t + 1 < n:
            a[t + 1] = softmax(items[t + 1], s.pop(t + 1))
        output(items[t], a.pop(t))


def _fox_attn_kernel(q_ref, k_ref, v_ref, c_ref, o_ref, va_ref, *, tq):
    n = q_ref.shape[0] // tq
    d = HEAD_DIM
    heads = q_ref.shape[1] // d
    mask = _causal_tile_mask(tq)
    for h in range(heads):
        va_ref[h, :, :d] = v_ref[:, h * d:(h + 1) * d]
        va_ref[h, :, d:] = jnp.ones((va_ref.shape[1], d), va_ref.dtype)

    def scores(item):
        h, i = item
        lo = i * tq
        cols = slice(h * d, (h + 1) * d)
        s = _nt_dot(q_ref[lo:lo + tq, cols], k_ref[:lo + tq, cols]) - c_ref[h, :, :lo + tq]
        diag = jnp.where(mask, s[:, lo:], MASK_VALUE)
        return [s[:, :lo], diag] if i > 0 else [diag]

    def softmax(item, parts):
        return _exp2_softmax_rows(parts, va_ref.dtype, with_sum=False)

    def output(item, p):
        h, i = item
        hi = (i + 1) * tq
        pv = jnp.dot(p, va_ref[h, :hi, :], preferred_element_type=jnp.float32)
        o_ref[hi - tq:hi, h * d:(h + 1) * d] = (pv[:, :d] / pv[:, d:]).astype(o_ref.dtype)

    _emit_pipelined(_pyramid_order(heads, n), scores, softmax, output)


def _fox_attention(proj, c, bsz, s):
    h, d, hs = FOX_HEADS, HEAD_DIM, FOX_HEADS_PER_STEP
    steps = h // hs
    w = hs * d
    kern = functools.partial(_fox_attn_kernel, tq=ATTN_TQ)
    return pl.pallas_call(
        kern,
        grid=(bsz, steps),
        in_specs=[pl.BlockSpec((None, s, w), lambda b, g: (b, 0, g)),
                  pl.BlockSpec((None, s, w), lambda b, g: (b, 0, steps + g)),
                  pl.BlockSpec((None, s, w), lambda b, g: (b, 0, 2 * steps + g)),
                  pl.BlockSpec((None, hs, 1, s), lambda b, g: (b, g, 0, 0))],
        out_specs=pl.BlockSpec((None, s, w), lambda b, g: (b, 0, g)),
        out_shape=jax.ShapeDtypeStruct((bsz, s, h * d), jnp.bfloat16),
        scratch_shapes=[pltpu.VMEM((hs, s, 2 * d), jnp.bfloat16)],
        compiler_params=_params("parallel", "parallel"),
        name="fox_attn",
    )(proj, proj, proj, c)


def _bias_kernel(tab_ref, o_ref):
    h = pl.program_id(0)
    r = pl.program_id(1)
    t = o_ref.shape[0]
    dq = lax.broadcasted_iota(jnp.int32, (t, t), 0)
    dk = lax.broadcasted_iota(jnp.int32, (t, t), 1)
    rel = r * t + dq - dk
    n = jnp.maximum(rel, 0)
    max_exact = NUM_BUCKETS // 2
    large = max_exact + (jnp.log(jnp.maximum(n, 1).astype(jnp.float32) / max_exact)
                         / math.log(MAX_DISTANCE / max_exact)
                         * (NUM_BUCKETS - max_exact)).astype(jnp.int32)
    large = jnp.minimum(large, NUM_BUCKETS - 1)
    bucket = jnp.where(n < max_exact, n, large)
    val = jnp.zeros((t, t), jnp.float32)
    for b in range(NUM_BUCKETS):
        val = jnp.where(bucket == b, tab_ref[b, h], val)
    val = (val - tab_ref[NUM_BUCKETS - 1, h]) * LOG2E
    o_ref[...] = jnp.where(rel >= 0, val, MASK_VALUE)


def _bias_tiles(rel_bias, t):
    nh = rel_bias.shape[1]
    return pl.pallas_call(
        _bias_kernel,
        grid=(nh, 2),
        in_specs=[pl.BlockSpec(memory_space=pltpu.SMEM)],
        out_specs=pl.BlockSpec((None, None, t, t), lambda h, r: (h, r, 0, 0)),
        out_shape=jax.ShapeDtypeStruct((nh, 2, t, t), jnp.float32),
        compiler_params=_params("parallel", "parallel"),
        name="bias_tiles",
    )(rel_bias)


def _diff_attn_kernel(q_ref, k_ref, v_ref, bias_ref, lq_ref, lk_ref, g_ref, o_ref,
                      *, tq, lambda_init):
    n = q_ref.shape[0] // tq
    d = HEAD_DIM
    lq = lq_ref[...]
    lk = lk_ref[...]
    dots = jnp.sum(lq * lk, axis=-1, keepdims=True)
    lam = jnp.exp(dots[0:1]) - jnp.exp(dots[1:2]) + lambda_init

    def scores(i):
        lo = i * tq
        maps = []
        for idx in range(2):
            cols = slice(idx * d, (idx + 1) * d)
            s = _nt_dot(q_ref[lo:lo + tq, cols], k_ref[:lo + tq, cols])
            parts = [s[:, :lo - tq]] if i > 1 else []
            if i > 0:
                parts.append(s[:, lo - tq:lo] + bias_ref[1])
            parts.append(s[:, lo:] + bias_ref[0])
            maps.append(parts)
        return maps

    def softmax(i, maps):
        p1, l1 = _exp2_softmax_rows(maps[0], v_ref.dtype, with_sum=True)
        p2, l2 = _exp2_softmax_rows(maps[1], v_ref.dtype, with_sum=True)
        return jnp.concatenate([p1, p2], axis=0), 1.0 / l1, lam / l2

    def output(i, args):
        p, r1, r2 = args
        hi = (i + 1) * tq
        pv = jnp.dot(p, v_ref[:hi, :], preferred_element_type=jnp.float32)
        o = pv[:tq] * r1 - pv[tq:] * r2
        o = o * lax.rsqrt(jnp.mean(o * o, axis=-1, keepdims=True) + RMS_EPS) * g_ref[...]
        o_ref[hi - tq:hi, :] = (o * (1.0 - lambda_init)).astype(o_ref.dtype)

    _emit_pipelined(list(range(n)), scores, softmax, output)


def _diff_attention(proj, bias, lam_q, lam_k, subln_g, bsz, s, lambda_init):
    t = ATTN_TQ
    h, d2 = DIFF_HEADS, 2 * HEAD_DIM
    kern = functools.partial(_diff_attn_kernel, tq=t, lambda_init=lambda_init)
    return pl.pallas_call(
        kern,
        grid=(bsz, h),
        in_specs=[pl.BlockSpec((None, s, d2), lambda b, hh: (b, 0, hh)),
                  pl.BlockSpec((None, s, d2), lambda b, hh: (b, 0, h + hh)),
                  pl.BlockSpec((None, s, d2), lambda b, hh: (b, 0, 2 * h + hh)),
                  pl.BlockSpec((None, 2, t, t), lambda b, hh: (hh, 0, 0, 0)),
                  pl.BlockSpec((2, HEAD_DIM), lambda b, hh: (0, 0)),
                  pl.BlockSpec((2, HEAD_DIM), lambda b, hh: (0, 0)),
                  pl.BlockSpec((1, d2), lambda b, hh: (0, 0))],
        out_specs=pl.BlockSpec((None, s, d2), lambda b, hh: (b, 0, hh)),
        out_shape=jax.ShapeDtypeStruct((bsz, s, h * d2), jnp.bfloat16),
        compiler_params=_params("parallel", "parallel"),
        name="diff_attn",
    )(proj, proj, proj, bias, lam_q, lam_k, subln_g.reshape(1, d2))


def _out_ln_kernel(o_ref, z_ref, x_ref, w_ref, g_ref, b_ref, y_ref, *maybe_yb_ref, alpha):
    o = o_ref[...].astype(jnp.float32)
    z = z_ref[...].astype(jnp.float32)
    gated = o * (z / (1.0 + jnp.exp(-z)))
    acc = jnp.dot(gated.astype(w_ref.dtype), w_ref[...], preferred_element_type=jnp.float32)
    v = alpha * x_ref[...] + acc
    mu = jnp.mean(v, axis=-1, keepdims=True)
    vc = v - mu
    var = jnp.mean(vc * vc, axis=-1, keepdims=True)
    y = vc * lax.rsqrt(var + LN_EPS) * g_ref[...] + b_ref[...]
    y_ref[...] = y
    for yb_ref in maybe_yb_ref:
        yb_ref[...] = y.astype(yb_ref.dtype)


def _out_ln(o2d, proj2d, x2d, w_bf16, g, b, alpha, emit_bf16):
    m, d = x2d.shape
    br = o2d.shape[1]
    tm = OUT_TM
    z_block = proj2d.shape[1] // br - 1
    kern = functools.partial(_out_ln_kernel, alpha=alpha)
    resident = dict(pipeline_mode=pl.Buffered(1))
    out_specs = [pl.BlockSpec((tm, d), lambda i: (i, 0))]
    out_shape = [jax.ShapeDtypeStruct((m, d), jnp.float32)]
    if emit_bf16:
        out_specs.append(pl.BlockSpec((tm, d), lambda i: (i, 0)))
        out_shape.append(jax.ShapeDtypeStruct((m, d), jnp.bfloat16))
    return pl.pallas_call(
        kern,
        grid=(m // tm,),
        in_specs=[pl.BlockSpec((tm, br), lambda i: (i, 0)),
                  pl.BlockSpec((tm, br), lambda i: (i, z_block)),
                  pl.BlockSpec((tm, d), lambda i: (i, 0)),
                  pl.BlockSpec((br, d), lambda i: (0, 0), **resident),
                  pl.BlockSpec((1, d), lambda i: (0, 0), **resident),
                  pl.BlockSpec((1, d), lambda i: (0, 0), **resident)],
        out_specs=out_specs,
        out_shape=out_shape,
        compiler_params=_params("parallel"),
        name="out_ln",
    )(o2d, proj2d, x2d, w_bf16, g.reshape(1, d), b.reshape(1, d))


def kernel(x, fox_w_in, fox_b_f, fox_w_out, diff_w_in, diff_lam_q, diff_lam_k,
           diff_subln_g, diff_w_out, rel_bias, ln_g, ln_b):
    bsz, s, d = x.shape
    depth = ln_g.shape[0]
    branch = fox_w_out.shape[1]
    alpha = (2 * depth) ** 0.25
    scale = HEAD_DIM ** -0.5 * LOG2E
    m = bsz * s

    x2d = x.reshape(m, d)
    xb2d = None
    for i in range(depth):
        j = i // 2
        if i % 2 == 0:
            w_in_t = fox_w_in[j].T
            c, xb = _forget_cumsum(x2d.reshape(bsz, s, d), w_in_t[4 * branch:], fox_b_f[j])
            xb2d = xb.reshape(m, d)
            proj, w_out_b = _project(xb2d, w_in_t, 4 * branch, branch, scale,
                                     w_transposed=True, w_out=fox_w_out[j])
            o = _fox_attention(proj.reshape(bsz, s, 4 * branch),
                               c.reshape(bsz, FOX_HEADS, 1, s), bsz, s)
        else:
            lambda_init = 0.8 - 0.6 * math.exp(-0.3 * i)
            proj, w_out_b = _project(xb2d, diff_w_in[j], 4 * branch, branch, scale,
                                     w_transposed=False, w_out=diff_w_out[j])
            bias = _bias_tiles(rel_bias, ATTN_TQ)
            o = _diff_attention(proj.reshape(bsz, s, 4 * branch), bias, diff_lam_q[j],
                                diff_lam_k[j], diff_subln_g[j], bsz, s, lambda_init)
        next_is_diff = i + 1 < depth and (i + 1) % 2 == 1
        outs = _out_ln(o.reshape(m, branch), proj, x2d, w_out_b,
                       ln_g[i], ln_b[i], alpha, emit_bf16=next_is_diff)
        x2d = outs[0]
        xb2d = outs[1] if next_is_diff else None
    return x2d.reshape(bsz, s, d)
```

```python
import functools
import math

import jax
import jax.numpy as jnp
from jax import lax
from jax.experimental import pallas as pl
from jax.experimental.pallas import tpu as pltpu

LANES = 128
HEAD_DIM = 128
FOX_HEADS = 16
DIFF_HEADS = 8
NUM_BUCKETS = 32
MAX_DISTANCE = 128
LN_EPS = 1e-5
RMS_EPS = 1e-5
MASK_VALUE = -1e30
LOG2E = math.log2(math.e)

V7X_VMEM_LIMIT = 56 * 1024 * 1024

PROJ_TM = 2048
PROJ_TN = 1024
FOX_Q_TS = 512
ATTN_TQ = 256
FOX_HEADS_PER_STEP = 4
OUT_TM = 512

_NT = (((1,), (1,)), ((), ()))


def _nt_dot(a, b):
    return lax.dot_general(a, b, _NT, preferred_element_type=jnp.float32)


def _params(*sem):
    return pltpu.CompilerParams(dimension_semantics=sem, vmem_limit_bytes=V7X_VMEM_LIMIT)


def _round_slab(wo_ref, wob_ref):
    wob_ref[...] = wo_ref[...].astype(wob_ref.dtype)


def _slab_spec(w_out, n_steps, step_index):
    return pl.BlockSpec((w_out.shape[0] // n_steps, w_out.shape[1]),
                        lambda *idx: (step_index(*idx), 0))


def _proj_kernel(x_ref, w_ref, *refs, n_scaled_tiles, scale, w_transposed, with_w_out):
    if with_w_out:
        wo_ref, o_ref, wob_ref, wb_ref = refs
        _round_slab(wo_ref, wob_ref)
    else:
        o_ref, wb_ref = refs

    @pl.when(pl.program_id(1) == 0)
    def _():
        wb_ref[...] = w_ref[...].astype(wb_ref.dtype)

    if w_transposed:
        acc = _nt_dot(x_ref[...], wb_ref[...])
    else:
        acc = jnp.dot(x_ref[...], wb_ref[...], preferred_element_type=jnp.float32)
    s = jnp.where(pl.program_id(0) < n_scaled_tiles, jnp.float32(scale), jnp.float32(1.0))
    o_ref[...] = (acc * s).astype(o_ref.dtype)


def _project(xb2d, w, col0, n, *, w_transposed, q_cols=0, scale=1.0, w_out=None):
    m, k = xb2d.shape
    tm, tn = PROJ_TM, PROJ_TN
    n_i, n_j, j0 = m // tm, n // tn, col0 // tn
    kern = functools.partial(_proj_kernel, n_scaled_tiles=q_cols // tn, scale=scale,
                             w_transposed=w_transposed, with_w_out=w_out is not None)
    if w_transposed:
        w_spec = pl.BlockSpec((tn, k), lambda j, i: (j0 + j, 0))
        wb_shape = (tn, k)
    else:
        w_spec = pl.BlockSpec((k, tn), lambda j, i: (0, j0 + j))
        wb_shape = (k, tn)
    in_specs = [pl.BlockSpec((tm, k), lambda j, i: (i, 0)), w_spec]
    out_specs = [pl.BlockSpec((tm, tn), lambda j, i: (i, j))]
    out_shape = [jax.ShapeDtypeStruct((m, n), jnp.bfloat16)]
    operands = [xb2d, w]
    if w_out is not None:
        wo_spec = _slab_spec(w_out, n_j * n_i, lambda j, i: j * n_i + i)
        in_specs.append(wo_spec)
        out_specs.append(wo_spec)
        out_shape.append(jax.ShapeDtypeStruct(w_out.shape, jnp.bfloat16))
        operands.append(w_out)
    outs = pl.pallas_call(
        kern,
        grid=(n_j, n_i),
        in_specs=in_specs,
        out_specs=out_specs,
        out_shape=out_shape,
        scratch_shapes=[pltpu.VMEM(wb_shape, jnp.bfloat16)],
        compiler_params=_params("parallel", "arbitrary"),
        name="in_proj",
    )(*operands)
    return outs if w_out is not None else outs[0]


def _split3(a):
    p1 = a.astype(jnp.bfloat16)
    r1 = a - p1.astype(jnp.float32)
    p2 = r1.astype(jnp.bfloat16)
    p3 = (r1 - p2.astype(jnp.float32)).astype(jnp.bfloat16)
    return p1, p2, p3


def _fox_q_kernel(x_ref, wq_ref, wt_ref, b_ref, wo_ref, xb_ref, q_ref, c_ref, wob_ref,
                  wqb_ref, carry_ref, *, scale):
    t = pl.program_id(1)
    ts = x_ref.shape[0]
    _round_slab(wo_ref, wob_ref)

    @pl.when((pl.program_id(0) == 0) & (t == 0))
    def _():
        wqb_ref[...] = wq_ref[...].astype(wqb_ref.dtype)

    @pl.when(t == 0)
    def _():
        carry_ref[...] = jnp.zeros_like(carry_ref)

    xb = x_ref[...].astype(jnp.bfloat16)
    xb_ref[...] = xb
    q_ref[...] = (_nt_dot(xb, wqb_ref[...]) * scale).astype(q_ref.dtype)
    f = _nt_dot(wt_ref[...].astype(jnp.bfloat16), xb)
    z = f + b_ref[...]
    logf = jnp.minimum(z, 0.0) - jnp.log1p(jnp.exp(-jnp.abs(z)))
    row = lax.broadcasted_iota(jnp.int32, (ts, ts), 0)
    col = lax.broadcasted_iota(jnp.int32, (ts, ts), 1)
    tri = (row <= col).astype(jnp.bfloat16)
    h = logf.shape[0]
    sums = jnp.dot(jnp.concatenate(_split3(logf), axis=0), tri,
                   preferred_element_type=jnp.float32)
    c = (sums[:h] + sums[h:2 * h]) + sums[2 * h:] + carry_ref[...]
    c_ref[...] = c * LOG2E
    carry_ref[...] = c[:, ts - 1:ts]


def _fox_q_pass(x, w_in_t, n_q, b_f, scale, w_out):
    bsz, s, d = x.shape
    h = b_f.shape[0]
    ts = FOX_Q_TS
    nt = s // ts
    gate_block = (w_in_t.shape[0] - h) // h
    once = dict(pipeline_mode=pl.Buffered(1))
    wo_spec = _slab_spec(w_out, bsz * nt, lambda b, t: b * nt + t)
    return pl.pallas_call(
        functools.partial(_fox_q_kernel, scale=scale),
        grid=(bsz, nt),
        in_specs=[pl.BlockSpec((None, ts, d), lambda b, t: (b, t, 0)),
                  pl.BlockSpec((n_q, d), lambda b, t: (0, 0), **once),
                  pl.BlockSpec((h, d), lambda b, t: (gate_block, 0), **once),
                  pl.BlockSpec((h, 1), lambda b, t: (0, 0), **once),
                  wo_spec],
        out_specs=[pl.BlockSpec((None, ts, d), lambda b, t: (b, t, 0)),
                   pl.BlockSpec((None, ts, n_q), lambda b, t: (b, t, 0)),
                   pl.BlockSpec((None, h, ts), lambda b, t: (b, 0, t)),
                   wo_spec],
        out_shape=[jax.ShapeDtypeStruct((bsz, s, d), jnp.bfloat16),
                   jax.ShapeDtypeStruct((bsz, s, n_q), jnp.bfloat16),
                   jax.ShapeDtypeStruct((bsz, h, s), jnp.float32),
                   jax.ShapeDtypeStruct(w_out.shape, jnp.bfloat16)],
        scratch_shapes=[pltpu.VMEM((n_q, d), jnp.bfloat16), pltpu.VMEM((h, 1), jnp.float32)],
        compiler_params=_params("arbitrary", "arbitrary"),
        name="fox_q_pass",
    )(x, w_in_t, w_in_t, b_f.reshape(h, 1), w_out)


def _causal_tile_mask(t):
    row = lax.broadcasted_iota(jnp.int32, (t, t), 0)
    col = lax.broadcasted_iota(jnp.int32, (t, t), 1)
    return col <= row


def _row_reduce(parts, combine, lane_reduce):
    tiles = [s[:, j:j + LANES] for s in parts for j in range(0, s.shape[1], LANES)]
    return lane_reduce(functools.reduce(combine, tiles), axis=-1, keepdims=True)


def _exp2_softmax_rows(parts, dtype, with_sum):
    m = _row_reduce(parts, jnp.maximum, jnp.max)
    ps = [jnp.exp2(s - m) for s in parts]
    p = jnp.concatenate([x.astype(dtype) for x in ps], axis=1)
    return (p, _row_reduce(ps, jnp.add, jnp.sum)) if with_sum else p


def _pyramid_order(heads, n):
    items = [(h, i) for i in range(n) for h in range(heads)]
    return items[0::2] + items[1::2][::-1]


def _emit_pipelined(items, scores, softmax, output):
    n = len(items)
    s = {0: scores(items[0])}
    if n > 1:
        s[1] = scores(items[1])
    a = {0: softmax(items[0], s.pop(0))}
    for t in range(n):
        if t + 2 < n:
            s[t + 2] = scores(items[t + 2])
        if t + 1 < n:
            a[t + 1] = softmax(items[t + 1], s.pop(t + 1))
        output(items[t], a.pop(t))


def _fox_attn_kernel(q_ref, k_ref, v_ref, c_ref, o_ref, va_ref, *, tq):
    n = q_ref.shape[0] // tq
    d = HEAD_DIM
    heads = q_ref.shape[1] // d
    mask = _causal_tile_mask(tq)
    for h in range(heads):
        va_ref[h, :, :d] = v_ref[:, h * d:(h + 1) * d]
        va_ref[h, :, d:] = jnp.ones((va_ref.shape[1], d), va_ref.dtype)

    def scores(item):
        h, i = item
        lo = i * tq
        cols = slice(h * d, (h + 1) * d)
        s = _nt_dot(q_ref[lo:lo + tq, cols], k_ref[:lo + tq, cols]) - c_ref[h, :, :lo + tq]
        diag = jnp.where(mask, s[:, lo:], MASK_VALUE)
        return [s[:, :lo], diag] if i > 0 else [diag]

    def softmax(item, parts):
        return _exp2_softmax_rows(parts, va_ref.dtype, with_sum=False)

    def output(item, p):
        h, i = item
        hi = (i + 1) * tq
        pv = jnp.dot(p, va_ref[h, :hi, :], preferred_element_type=jnp.float32)
        o_ref[hi - tq:hi, h * d:(h + 1) * d] = (pv[:, :d] / pv[:, d:]).astype(o_ref.dtype)

    _emit_pipelined(_pyramid_order(heads, n), scores, softmax, output)


def _fox_attention(q, kvz, c):
    bsz, s, _ = q.shape
    h, d, hs = FOX_HEADS, HEAD_DIM, FOX_HEADS_PER_STEP
    steps = h // hs
    w = hs * d
    kern = functools.partial(_fox_attn_kernel, tq=ATTN_TQ)
    return pl.pallas_call(
        kern,
        grid=(bsz, steps),
        in_specs=[pl.BlockSpec((None, s, w), lambda b, g: (b, 0, g)),
                  pl.BlockSpec((None, s, w), lambda b, g: (b, 0, g)),
                  pl.BlockSpec((None, s, w), lambda b, g: (b, 0, steps + g)),
                  pl.BlockSpec((None, hs, 1, s), lambda b, g: (b, g, 0, 0))],
        out_specs=pl.BlockSpec((None, s, w), lambda b, g: (b, 0, g)),
        out_shape=jax.ShapeDtypeStruct((bsz, s, h * d), jnp.bfloat16),
        scratch_shapes=[pltpu.VMEM((hs, s, 2 * d), jnp.bfloat16)],
        compiler_params=_params("parallel", "parallel"),
        name="fox_attn",
    )(q, kvz, kvz, c)


def _bias_kernel(tab_ref, o_ref):
    h = pl.program_id(0)
    r = pl.program_id(1)
    t = o_ref.shape[0]
    dq = lax.broadcasted_iota(jnp.int32, (t, t), 0)
    dk = lax.broadcasted_iota(jnp.int32, (t, t), 1)
    rel = r * t + dq - dk
    n = jnp.maximum(rel, 0)
    max_exact = NUM_BUCKETS // 2
    large = max_exact + (jnp.log(jnp.maximum(n, 1).astype(jnp.float32) / max_exact)
                         / math.log(MAX_DISTANCE / max_exact)
                         * (NUM_BUCKETS - max_exact)).astype(jnp.int32)
    large = jnp.minimum(large, NUM_BUCKETS - 1)
    bucket = jnp.where(n < max_exact, n, large)
    val = jnp.zeros((t, t), jnp.float32)
    for b in range(NUM_BUCKETS):
        val = jnp.where(bucket == b, tab_ref[b, h], val)
    val = (val - tab_ref[NUM_BUCKETS - 1, h]) * LOG2E
    o_ref[...] = jnp.where(rel >= 0, val, MASK_VALUE)


def _bias_tiles(rel_bias, t):
    nh = rel_bias.shape[1]
    return pl.pallas_call(
        _bias_kernel,
        grid=(nh, 2),
        in_specs=[pl.BlockSpec(memory_space=pltpu.SMEM)],
        out_specs=pl.BlockSpec((None, None, t, t), lambda h, r: (h, r, 0, 0)),
        out_shape=jax.ShapeDtypeStruct((nh, 2, t, t), jnp.float32),
        compiler_params=_params("parallel", "parallel"),
        name="bias_tiles",
    )(rel_bias)


def _diff_attn_kernel(q_ref, k_ref, v_ref, bias_ref, lq_ref, lk_ref, g_ref, o_ref,
                      *, tq, lambda_init):
    n = q_ref.shape[0] // tq
    d = HEAD_DIM
    lq = lq_ref[...]
    lk = lk_ref[...]
    dots = jnp.sum(lq * lk, axis=-1, keepdims=True)
    lam = jnp.exp(dots[0:1]) - jnp.exp(dots[1:2]) + lambda_init

    def scores(i):
        lo = i * tq
        maps = []
        for idx in range(2):
            cols = slice(idx * d, (idx + 1) * d)
            s = _nt_dot(q_ref[lo:lo + tq, cols], k_ref[:lo + tq, cols])
            parts = [s[:, :lo - tq]] if i > 1 else []
            if i > 0:
                parts.append(s[:, lo - tq:lo] + bias_ref[1])
            parts.append(s[:, lo:] + bias_ref[0])
            maps.append(parts)
        return maps

    def softmax(i, maps):
        p1, l1 = _exp2_softmax_rows(maps[0], v_ref.dtype, with_sum=True)
        p2, l2 = _exp2_softmax_rows(maps[1], v_ref.dtype, with_sum=True)
        return jnp.concatenate([p1, p2], axis=0), 1.0 / l1, lam / l2

    def output(i, args):
        p, r1, r2 = args
        hi = (i + 1) * tq
        pv = jnp.dot(p, v_ref[:hi, :], preferred_element_type=jnp.float32)
        o = pv[:tq] * r1 - pv[tq:] * r2
        o = o * lax.rsqrt(jnp.mean(o * o, axis=-1, keepdims=True) + RMS_EPS) * g_ref[...]
        o_ref[hi - tq:hi, :] = (o * (1.0 - lambda_init)).astype(o_ref.dtype)

    _emit_pipelined(list(range(n)), scores, softmax, output)


def _diff_attention(proj, bias, lam_q, lam_k, subln_g, lambda_init):
    bsz, s, _ = proj.shape
    t = ATTN_TQ
    h, d2 = DIFF_HEADS, 2 * HEAD_DIM
    kern = functools.partial(_diff_attn_kernel, tq=t, lambda_init=lambda_init)
    return pl.pallas_call(
        kern,
        grid=(bsz, h),
        in_specs=[pl.BlockSpec((None, s, d2), lambda b, hh: (b, 0, hh)),
                  pl.BlockSpec((None, s, d2), lambda b, hh: (b, 0, h + hh)),
                  pl.BlockSpec((None, s, d2), lambda b, hh: (b, 0, 2 * h + hh)),
                  pl.BlockSpec((None, 2, t, t), lambda b, hh: (hh, 0, 0, 0)),
                  pl.BlockSpec((2, HEAD_DIM), lambda b, hh: (0, 0)),
                  pl.BlockSpec((2, HEAD_DIM), lambda b, hh: (0, 0)),
                  pl.BlockSpec((1, d2), lambda b, hh: (0, 0))],
        out_specs=pl.BlockSpec((None, s, d2), lambda b, hh: (b, 0, hh)),
        out_shape=jax.ShapeDtypeStruct((bsz, s, h * d2), jnp.bfloat16),
        compiler_params=_params("parallel", "parallel"),
        name="diff_attn",
    )(proj, proj, proj, bias, lam_q, lam_k, subln_g.reshape(1, d2))


def _out_ln_kernel(o_ref, z_ref, x_ref, w_ref, g_ref, b_ref, y_ref, *maybe_yb_ref, alpha):
    o = o_ref[...].astype(jnp.float32)
    z = z_ref[...].astype(jnp.float32)
    gated = o * (z / (1.0 + jnp.exp(-z)))
    acc = jnp.dot(gated.astype(w_ref.dtype), w_ref[...], preferred_element_type=jnp.float32)
    v = alpha * x_ref[...] + acc
    mu = jnp.mean(v, axis=-1, keepdims=True)
    vc = v - mu
    var = jnp.mean(vc * vc, axis=-1, keepdims=True)
    y = vc * lax.rsqrt(var + LN_EPS) * g_ref[...] + b_ref[...]
    y_ref[...] = y
    for yb_ref in maybe_yb_ref:
        yb_ref[...] = y.astype(yb_ref.dtype)


def _out_ln(o2d, proj2d, x2d, w_bf16, g, b, alpha, emit_bf16):
    m, d = x2d.shape
    br = o2d.shape[1]
    tm = OUT_TM
    z_block = proj2d.shape[1] // br - 1
    kern = functools.partial(_out_ln_kernel, alpha=alpha)
    resident = dict(pipeline_mode=pl.Buffered(1))
    out_specs = [pl.BlockSpec((tm, d), lambda i: (i, 0))]
    out_shape = [jax.ShapeDtypeStruct((m, d), jnp.float32)]
    if emit_bf16:
        out_specs.append(pl.BlockSpec((tm, d), lambda i: (i, 0)))
        out_shape.append(jax.ShapeDtypeStruct((m, d), jnp.bfloat16))
    return pl.pallas_call(
        kern,
        grid=(m // tm,),
        in_specs=[pl.BlockSpec((tm, br), lambda i: (i, 0)),
                  pl.BlockSpec((tm, br), lambda i: (i, z_block)),
                  pl.BlockSpec((tm, d), lambda i: (i, 0)),
                  pl.BlockSpec((br, d), lambda i: (0, 0), **resident),
                  pl.BlockSpec((1, d), lambda i: (0, 0), **resident),
                  pl.BlockSpec((1, d), lambda i: (0, 0), **resident)],
        out_specs=out_specs,
        out_shape=out_shape,
        compiler_params=_params("parallel"),
        name="out_ln",
    )(o2d, proj2d, x2d, w_bf16, g.reshape(1, d), b.reshape(1, d))


def kernel(x, fox_w_in, fox_b_f, fox_w_out, diff_w_in, diff_lam_q, diff_lam_k,
           diff_subln_g, diff_w_out, rel_bias, ln_g, ln_b):
    bsz, s, d = x.shape
    depth = ln_g.shape[0]
    branch = fox_w_out.shape[1]
    alpha = (2 * depth) ** 0.25
    scale = HEAD_DIM ** -0.5 * LOG2E
    m = bsz * s

    x2d = x.reshape(m, d)
    xb2d = None
    for i in range(depth):
        j = i // 2
        if i % 2 == 0:
            w_in_t = fox_w_in[j].T
            xb, q, c, w_out_b = _fox_q_pass(x2d.reshape(bsz, s, d), w_in_t, branch, fox_b_f[j],
                                            scale, fox_w_out[j])
            proj = _project(xb.reshape(m, d), w_in_t, branch, 3 * branch, w_transposed=True)
            o = _fox_attention(q, proj.reshape(bsz, s, 3 * branch),
                               c.reshape(bsz, FOX_HEADS, 1, s))
        else:
            lambda_init = 0.8 - 0.6 * math.exp(-0.3 * i)
            proj, w_out_b = _project(xb2d, diff_w_in[j], 0, 4 * branch, w_transposed=False,
                                     q_cols=branch, scale=scale, w_out=diff_w_out[j])
            bias = _bias_tiles(rel_bias, ATTN_TQ)
            o = _diff_attention(proj.reshape(bsz, s, 4 * branch), bias, diff_lam_q[j],
                                diff_lam_k[j], diff_subln_g[j], lambda_init)
        next_is_diff = i + 1 < depth and (i + 1) % 2 == 1
        outs = _out_ln(o.reshape(m, branch), proj, x2d, w_out_b,
                       ln_g[i], ln_b[i], alpha, emit_bf16=next_is_diff)
        x2d = outs[0]
        xb2d = outs[1] if next_is_diff else None
    return x2d.reshape(bsz, s, d)
```

```python
import functools
import math

import jax
import jax.numpy as jnp
from jax import lax
from jax.experimental import pallas as pl
from jax.experimental.pallas import tpu as pltpu

LANES = 128
HEAD_DIM = 128
FOX_HEADS = 16
DIFF_HEADS = 8
NUM_BUCKETS = 32
MAX_DISTANCE = 128
LN_EPS = 1e-5
RMS_EPS = 1e-5
MASK_VALUE = -1e30
LOG2E = math.log2(math.e)

V7X_VMEM_LIMIT = 56 * 1024 * 1024

PROJ_TM = 2048
PROJ_TN = 1024
FOX_Q_TS = 512
ATTN_TQ = 256
FOX_HEADS_PER_STEP = 4
OUT_TM = 512
OUT_N_CHUNKS = 4

_NT = (((1,), (1,)), ((), ()))


def _nt_dot(a, b):
    return lax.dot_general(a, b, _NT, preferred_element_type=jnp.float32)


def _silu(z):
    return z / (1.0 + jnp.exp(-z))


def _params(*sem):
    return pltpu.CompilerParams(dimension_semantics=sem, vmem_limit_bytes=V7X_VMEM_LIMIT)


def _round_slab(wo_ref, wob_ref):
    wob_ref[...] = wo_ref[...].astype(wob_ref.dtype)


def _slab_spec(w_out, n_steps, step_index):
    return pl.BlockSpec((w_out.shape[0] // n_steps, w_out.shape[1]),
                        lambda *idx: (step_index(*idx), 0))


def _proj_kernel(x_ref, w_ref, *refs, n_scaled_tiles, scale, w_transposed, with_w_out):
    if with_w_out:
        wo_ref, o_ref, wob_ref, wb_ref = refs
        _round_slab(wo_ref, wob_ref)
    else:
        o_ref, wb_ref = refs

    @pl.when(pl.program_id(1) == 0)
    def _():
        wb_ref[...] = w_ref[...].astype(wb_ref.dtype)

    if w_transposed:
        acc = _nt_dot(x_ref[...], wb_ref[...])
    else:
        acc = jnp.dot(x_ref[...], wb_ref[...], preferred_element_type=jnp.float32)
    s = jnp.where(pl.program_id(0) < n_scaled_tiles, jnp.float32(scale), jnp.float32(1.0))
    o_ref[...] = (acc * s).astype(o_ref.dtype)


def _project(xb2d, w, col0, n, *, w_transposed, q_cols=0, scale=1.0, w_out=None):
    m, k = xb2d.shape
    tm, tn = PROJ_TM, PROJ_TN
    n_i, n_j, j0 = m // tm, n // tn, col0 // tn
    kern = functools.partial(_proj_kernel, n_scaled_tiles=q_cols // tn, scale=scale,
                             w_transposed=w_transposed, with_w_out=w_out is not None)
    if w_transposed:
        w_spec = pl.BlockSpec((tn, k), lambda j, i: (j0 + j, 0))
        wb_shape = (tn, k)
    else:
        w_spec = pl.BlockSpec((k, tn), lambda j, i: (0, j0 + j))
        wb_shape = (k, tn)
    in_specs = [pl.BlockSpec((tm, k), lambda j, i: (i, 0)), w_spec]
    out_specs = [pl.BlockSpec((tm, tn), lambda j, i: (i, j))]
    out_shape = [jax.ShapeDtypeStruct((m, n), jnp.bfloat16)]
    operands = [xb2d, w]
    if w_out is not None:
        wo_spec = _slab_spec(w_out, n_j * n_i, lambda j, i: j * n_i + i)
        in_specs.append(wo_spec)
        out_specs.append(wo_spec)
        out_shape.append(jax.ShapeDtypeStruct(w_out.shape, jnp.bfloat16))
        operands.append(w_out)
    outs = pl.pallas_call(
        kern,
        grid=(n_j, n_i),
        in_specs=in_specs,
        out_specs=out_specs,
        out_shape=out_shape,
        scratch_shapes=[pltpu.VMEM(wb_shape, jnp.bfloat16)],
        compiler_params=_params("parallel", "arbitrary"),
        name="in_proj",
    )(*operands)
    return outs if w_out is not None else outs[0]


def _split3(a):
    p1 = a.astype(jnp.bfloat16)
    r1 = a - p1.astype(jnp.float32)
    p2 = r1.astype(jnp.bfloat16)
    p3 = (r1 - p2.astype(jnp.float32)).astype(jnp.bfloat16)
    return p1, p2, p3


def _fox_q_kernel(x_ref, wq_ref, wt_ref, b_ref, wo_ref, xb_ref, q_ref, c_ref, wob_ref,
                  wqb_ref, carry_ref, *, scale):
    t = pl.program_id(1)
    ts = x_ref.shape[0]
    _round_slab(wo_ref, wob_ref)

    @pl.when((pl.program_id(0) == 0) & (t == 0))
    def _():
        wqb_ref[...] = wq_ref[...].astype(wqb_ref.dtype)

    @pl.when(t == 0)
    def _():
        carry_ref[...] = jnp.zeros_like(carry_ref)

    xb = x_ref[...].astype(jnp.bfloat16)
    xb_ref[...] = xb
    q_ref[...] = (_nt_dot(xb, wqb_ref[...]) * scale).astype(q_ref.dtype)
    f = _nt_dot(wt_ref[...].astype(jnp.bfloat16), xb)
    z = f + b_ref[...]
    logf = jnp.minimum(z, 0.0) - jnp.log1p(jnp.exp(-jnp.abs(z)))
    row = lax.broadcasted_iota(jnp.int32, (ts, ts), 0)
    col = lax.broadcasted_iota(jnp.int32, (ts, ts), 1)
    tri = (row <= col).astype(jnp.bfloat16)
    h = logf.shape[0]
    sums = jnp.dot(jnp.concatenate(_split3(logf), axis=0), tri,
                   preferred_element_type=jnp.float32)
    c = (sums[:h] + sums[h:2 * h]) + sums[2 * h:] + carry_ref[...]
    c_ref[...] = c * LOG2E
    carry_ref[...] = c[:, ts - 1:ts]


def _fox_q_pass(x, w_in_t, n_q, b_f, scale, w_out):
    bsz, s, d = x.shape
    h = b_f.shape[0]
    ts = FOX_Q_TS
    nt = s // ts
    gate_block = (w_in_t.shape[0] - h) // h
    once = dict(pipeline_mode=pl.Buffered(1))
    wo_spec = _slab_spec(w_out, bsz * nt, lambda b, t: b * nt + t)
    return pl.pallas_call(
        functools.partial(_fox_q_kernel, scale=scale),
        grid=(bsz, nt),
        in_specs=[pl.BlockSpec((None, ts, d), lambda b, t: (b, t, 0)),
                  pl.BlockSpec((n_q, d), lambda b, t: (0, 0), **once),
                  pl.BlockSpec((h, d), lambda b, t: (gate_block, 0), **once),
                  pl.BlockSpec((h, 1), lambda b, t: (0, 0), **once),
                  wo_spec],
        out_specs=[pl.BlockSpec((None, ts, d), lambda b, t: (b, t, 0)),
                   pl.BlockSpec((None, ts, n_q), lambda b, t: (b, t, 0)),
                   pl.BlockSpec((None, h, ts), lambda b, t: (b, 0, t)),
                   wo_spec],
        out_shape=[jax.ShapeDtypeStruct((bsz, s, d), jnp.bfloat16),
                   jax.ShapeDtypeStruct((bsz, s, n_q), jnp.bfloat16),
                   jax.ShapeDtypeStruct((bsz, h, s), jnp.float32),
                   jax.ShapeDtypeStruct(w_out.shape, jnp.bfloat16)],
        scratch_shapes=[pltpu.VMEM((n_q, d), jnp.bfloat16), pltpu.VMEM((h, 1), jnp.float32)],
        compiler_params=_params("arbitrary", "arbitrary"),
        name="fox_q_pass",
    )(x, w_in_t, w_in_t, b_f.reshape(h, 1), w_out)


def _causal_tile_mask(t):
    row = lax.broadcasted_iota(jnp.int32, (t, t), 0)
    col = lax.broadcasted_iota(jnp.int32, (t, t), 1)
    return col <= row


def _row_reduce(parts, combine, lane_reduce):
    tiles = [s[:, j:j + LANES] for s in parts for j in range(0, s.shape[1], LANES)]
    return lane_reduce(functools.reduce(combine, tiles), axis=-1, keepdims=True)


def _exp2_softmax_rows(parts, dtype, with_sum):
    m = _row_reduce(parts, jnp.maximum, jnp.max)
    ps = [jnp.exp2(s - m) for s in parts]
    p = jnp.concatenate([x.astype(dtype) for x in ps], axis=1)
    return (p, _row_reduce(ps, jnp.add, jnp.sum)) if with_sum else p


def _pyramid_order(heads, n):
    items = [(h, i) for i in range(n) for h in range(heads)]
    return items[0::2] + items[1::2][::-1]


def _emit_pipelined(items, scores, softmax, output):
    n = len(items)
    s = {0: scores(items[0])}
    if n > 1:
        s[1] = scores(items[1])
    a = {0: softmax(items[0], s.pop(0))}
    for t in range(n):
        if t + 2 < n:
            s[t + 2] = scores(items[t + 2])
        if t + 1 < n:
            a[t + 1] = softmax(items[t + 1], s.pop(t + 1))
        output(items[t], a.pop(t))


def _fox_attn_kernel(q_ref, k_ref, v_ref, z_ref, c_ref, o_ref, va_ref, *, tq):
    n = q_ref.shape[0] // tq
    d = HEAD_DIM
    heads = q_ref.shape[1] // d
    mask = _causal_tile_mask(tq)
    for h in range(heads):
        va_ref[h, :, :d] = v_ref[:, h * d:(h + 1) * d]
        va_ref[h, :, d:] = jnp.ones((va_ref.shape[1], d), va_ref.dtype)

    def scores(item):
        h, i = item
        lo = i * tq
        cols = slice(h * d, (h + 1) * d)
        s = _nt_dot(q_ref[lo:lo + tq, cols], k_ref[:lo + tq, cols]) - c_ref[h, :, :lo + tq]
        diag = jnp.where(mask, s[:, lo:], MASK_VALUE)
        return [s[:, :lo], diag] if i > 0 else [diag]

    def softmax(item, parts):
        return _exp2_softmax_rows(parts, va_ref.dtype, with_sum=False)

    def output(item, p):
        h, i = item
        hi = (i + 1) * tq
        pv = jnp.dot(p, va_ref[h, :hi, :], preferred_element_type=jnp.float32)
        cols = slice(h * d, (h + 1) * d)
        gate = _silu(z_ref[hi - tq:hi, cols].astype(jnp.float32))
        o_ref[hi - tq:hi, cols] = (pv[:, :d] / pv[:, d:] * gate).astype(o_ref.dtype)

    _emit_pipelined(_pyramid_order(heads, n), scores, softmax, output)


def _fox_attention(q, kvz, c):
    bsz, s, _ = q.shape
    h, d, hs = FOX_HEADS, HEAD_DIM, FOX_HEADS_PER_STEP
    steps = h // hs
    w = hs * d
    kern = functools.partial(_fox_attn_kernel, tq=ATTN_TQ)
    return pl.pallas_call(
        kern,
        grid=(bsz, steps),
        in_specs=[pl.BlockSpec((None, s, w), lambda b, g: (b, 0, g)),
                  pl.BlockSpec((None, s, w), lambda b, g: (b, 0, g)),
                  pl.BlockSpec((None, s, w), lambda b, g: (b, 0, steps + g)),
                  pl.BlockSpec((None, s, w), lambda b, g: (b, 0, 2 * steps + g)),
                  pl.BlockSpec((None, hs, 1, s), lambda b, g: (b, g, 0, 0))],
        out_specs=pl.BlockSpec((None, s, w), lambda b, g: (b, 0, g)),
        out_shape=jax.ShapeDtypeStruct((bsz, s, h * d), jnp.bfloat16),
        scratch_shapes=[pltpu.VMEM((hs, s, 2 * d), jnp.bfloat16)],
        compiler_params=_params("parallel", "parallel"),
        name="fox_attn",
    )(q, kvz, kvz, kvz, c)


def _bias_kernel(tab_ref, o_ref):
    h = pl.program_id(0)
    r = pl.program_id(1)
    t = o_ref.shape[0]
    dq = lax.broadcasted_iota(jnp.int32, (t, t), 0)
    dk = lax.broadcasted_iota(jnp.int32, (t, t), 1)
    rel = r * t + dq - dk
    n = jnp.maximum(rel, 0)
    max_exact = NUM_BUCKETS // 2
    large = max_exact + (jnp.log(jnp.maximum(n, 1).astype(jnp.float32) / max_exact)
                         / math.log(MAX_DISTANCE / max_exact)
                         * (NUM_BUCKETS - max_exact)).astype(jnp.int32)
    large = jnp.minimum(large, NUM_BUCKETS - 1)
    bucket = jnp.where(n < max_exact, n, large)
    val = jnp.zeros((t, t), jnp.float32)
    for b in range(NUM_BUCKETS):
        val = jnp.where(bucket == b, tab_ref[b, h], val)
    val = (val - tab_ref[NUM_BUCKETS - 1, h]) * LOG2E
    o_ref[...] = jnp.where(rel >= 0, val, MASK_VALUE)


def _bias_tiles(rel_bias, t):
    nh = rel_bias.shape[1]
    return pl.pallas_call(
        _bias_kernel,
        grid=(nh, 2),
        in_specs=[pl.BlockSpec(memory_space=pltpu.SMEM)],
        out_specs=pl.BlockSpec((None, None, t, t), lambda h, r: (h, r, 0, 0)),
        out_shape=jax.ShapeDtypeStruct((nh, 2, t, t), jnp.float32),
        compiler_params=_params("parallel", "parallel"),
        name="bias_tiles",
    )(rel_bias)


def _diff_attn_kernel(q_ref, k_ref, v_ref, bias_ref, lq_ref, lk_ref, g_ref, o_ref,
                      *, tq, lambda_init):
    n = q_ref.shape[0] // tq
    d = HEAD_DIM
    lq = lq_ref[...]
    lk = lk_ref[...]
    dots = jnp.sum(lq * lk, axis=-1, keepdims=True)
    lam = jnp.exp(dots[0:1]) - jnp.exp(dots[1:2]) + lambda_init

    def scores(i):
        lo = i * tq
        maps = []
        for idx in range(2):
            cols = slice(idx * d, (idx + 1) * d)
            s = _nt_dot(q_ref[lo:lo + tq, cols], k_ref[:lo + tq, cols])
            parts = [s[:, :lo - tq]] if i > 1 else []
            if i > 0:
                parts.append(s[:, lo - tq:lo] + bias_ref[1])
            parts.append(s[:, lo:] + bias_ref[0])
            maps.append(parts)
        return maps

    def softmax(i, maps):
        p1, l1 = _exp2_softmax_rows(maps[0], v_ref.dtype, with_sum=True)
        p2, l2 = _exp2_softmax_rows(maps[1], v_ref.dtype, with_sum=True)
        return jnp.concatenate([p1, p2], axis=0), 1.0 / l1, lam / l2

    def output(i, args):
        p, r1, r2 = args
        hi = (i + 1) * tq
        pv = jnp.dot(p, v_ref[:hi, :], preferred_element_type=jnp.float32)
        o = pv[:tq] * r1 - pv[tq:] * r2
        o = o * lax.rsqrt(jnp.mean(o * o, axis=-1, keepdims=True) + RMS_EPS) * g_ref[...]
        o_ref[hi - tq:hi, :] = (o * (1.0 - lambda_init)).astype(o_ref.dtype)

    _emit_pipelined(list(range(n)), scores, softmax, output)


def _diff_attention(proj, bias, lam_q, lam_k, subln_g, lambda_init):
    bsz, s, _ = proj.shape
    t = ATTN_TQ
    h, d2 = DIFF_HEADS, 2 * HEAD_DIM
    kern = functools.partial(_diff_attn_kernel, tq=t, lambda_init=lambda_init)
    return pl.pallas_call(
        kern,
        grid=(bsz, h),
        in_specs=[pl.BlockSpec((None, s, d2), lambda b, hh: (b, 0, hh)),
                  pl.BlockSpec((None, s, d2), lambda b, hh: (b, 0, h + hh)),
                  pl.BlockSpec((None, s, d2), lambda b, hh: (b, 0, 2 * h + hh)),
                  pl.BlockSpec((None, 2, t, t), lambda b, hh: (hh, 0, 0, 0)),
                  pl.BlockSpec((2, HEAD_DIM), lambda b, hh: (0, 0)),
                  pl.BlockSpec((2, HEAD_DIM), lambda b, hh: (0, 0)),
                  pl.BlockSpec((1, d2), lambda b, hh: (0, 0))],
        out_specs=pl.BlockSpec((None, s, d2), lambda b, hh: (b, 0, hh)),
        out_shape=jax.ShapeDtypeStruct((bsz, s, h * d2), jnp.bfloat16),
        compiler_params=_params("parallel", "parallel"),
        name="diff_attn",
    )(proj, proj, proj, bias, lam_q, lam_k, subln_g.reshape(1, d2))


def _out_ln_kernel(*refs, alpha, gate_here):
    if gate_here:
        o_ref, z_ref, x_ref, w_ref, g_ref, b_ref, y_ref, *maybe_yb_ref = refs
        gated = o_ref[...].astype(jnp.float32) * _silu(z_ref[...].astype(jnp.float32))
        gated = gated.astype(w_ref.dtype)
    else:
        o_ref, x_ref, w_ref, g_ref, b_ref, y_ref, *maybe_yb_ref = refs
        gated = o_ref[...]
    d = w_ref.shape[1]
    cw = d // OUT_N_CHUNKS
    chunks = [slice(c, c + cw) for c in range(0, d, cw)]
    vs = [alpha * x_ref[:, c] + jnp.dot(gated, w_ref[:, c], preferred_element_type=jnp.float32)
          for c in chunks]
    mu = sum(jnp.sum(v, axis=-1, keepdims=True) for v in vs) / d
    vcs = [v - mu for v in vs]
    var = sum(jnp.sum(vc * vc, axis=-1, keepdims=True) for vc in vcs) / d
    rs = lax.rsqrt(var + LN_EPS)
    for c, vc in zip(chunks, vcs):
        y = vc * rs * g_ref[:, c] + b_ref[:, c]
        y_ref[:, c] = y
        for yb_ref in maybe_yb_ref:
            yb_ref[:, c] = y.astype(yb_ref.dtype)


def _out_ln(o2d, z_src, x2d, w_bf16, g, b, alpha, emit_bf16):
    m, d = x2d.shape
    br = o2d.shape[1]
    tm = OUT_TM
    kern = functools.partial(_out_ln_kernel, alpha=alpha, gate_here=z_src is not None)
    resident = dict(pipeline_mode=pl.Buffered(1))
    in_specs = [pl.BlockSpec((tm, br), lambda i: (i, 0))]
    operands = [o2d]
    if z_src is not None:
        z_block = z_src.shape[1] // br - 1
        in_specs.append(pl.BlockSpec((tm, br), lambda i: (i, z_block)))
        operands.append(z_src)
    in_specs += [pl.BlockSpec((tm, d), lambda i: (i, 0)),
                 pl.BlockSpec((br, d), lambda i: (0, 0), **resident),
                 pl.BlockSpec((1, d), lambda i: (0, 0), **resident),
                 pl.BlockSpec((1, d), lambda i: (0, 0), **resident)]
    operands += [x2d, w_bf16, g.reshape(1, d), b.reshape(1, d)]
    out_specs = [pl.BlockSpec((tm, d), lambda i: (i, 0))]
    out_shape = [jax.ShapeDtypeStruct((m, d), jnp.float32)]
    if emit_bf16:
        out_specs.append(pl.BlockSpec((tm, d), lambda i: (i, 0)))
        out_shape.append(jax.ShapeDtypeStruct((m, d), jnp.bfloat16))
    return pl.pallas_call(
        kern,
        grid=(m // tm,),
        in_specs=in_specs,
        out_specs=out_specs,
        out_shape=out_shape,
        compiler_params=_params("parallel"),
        name="out_ln",
    )(*operands)


def kernel(x, fox_w_in, fox_b_f, fox_w_out, diff_w_in, diff_lam_q, diff_lam_k,
           diff_subln_g, diff_w_out, rel_bias, ln_g, ln_b):
    bsz, s, d = x.shape
    depth = ln_g.shape[0]
    branch = fox_w_out.shape[1]
    alpha = (2 * depth) ** 0.25
    scale = HEAD_DIM ** -0.5 * LOG2E
    m = bsz * s

    x2d = x.reshape(m, d)
    xb2d = None
    for i in range(depth):
        j = i // 2
        if i % 2 == 0:
            w_in_t = fox_w_in[j].T
            xb, q, c, w_out_b = _fox_q_pass(x2d.reshape(bsz, s, d), w_in_t, branch, fox_b_f[j],
                                            scale, fox_w_out[j])
            proj = _project(xb.reshape(m, d), w_in_t, branch, 3 * branch, w_transposed=True)
            o = _fox_attention(q, proj.reshape(bsz, s, 3 * branch),
                               c.reshape(bsz, FOX_HEADS, 1, s))
        else:
            lambda_init = 0.8 - 0.6 * math.exp(-0.3 * i)
            proj, w_out_b = _project(xb2d, diff_w_in[j], 0, 4 * branch, w_transposed=False,
                                     q_cols=branch, scale=scale, w_out=diff_w_out[j])
            bias = _bias_tiles(rel_bias, ATTN_TQ)
            o = _diff_attention(proj.reshape(bsz, s, 4 * branch), bias, diff_lam_q[j],
                                diff_lam_k[j], diff_subln_g[j], lambda_init)
        next_is_diff = i + 1 < depth and (i + 1) % 2 == 1
        outs = _out_ln(o.reshape(m, branch), None if i % 2 == 0 else proj, x2d, w_out_b,
                       ln_g[i], ln_b[i], alpha, emit_bf16=next_is_diff)
        x2d = outs[0]
        xb2d = outs[1] if next_is_diff else None
    return x2d.reshape(bsz, s, d)
```

```python
import functools
import math

import jax
import jax.numpy as jnp
from jax import lax
from jax.experimental import pallas as pl
from jax.experimental.pallas import tpu as pltpu

LANES = 128
HEAD_DIM = 128
FOX_HEADS = 16
DIFF_HEADS = 8
NUM_BUCKETS = 32
MAX_DISTANCE = 128
LN_EPS = 1e-5
RMS_EPS = 1e-5
MASK_VALUE = -1e30
LOG2E = math.log2(math.e)

V7X_VMEM_LIMIT = 56 * 1024 * 1024

PROJ_TM = 2048
PROJ_TN = 1024
FOX_Q_TS = 512
ATTN_TQ = 256
FOX_HEADS_PER_STEP = 4
OUT_TM = 512
OUT_N_CHUNKS = 4

_NT = (((1,), (1,)), ((), ()))


def _nt_dot(a, b):
    return lax.dot_general(a, b, _NT, preferred_element_type=jnp.float32)


def _silu(z):
    return z / (1.0 + jnp.exp(-z))


def _params(*sem):
    return pltpu.CompilerParams(dimension_semantics=sem, vmem_limit_bytes=V7X_VMEM_LIMIT)


def _round_slab(wo_ref, wob_ref):
    wob_ref[...] = wo_ref[...].astype(wob_ref.dtype)


def _slab_spec(w_out, n_steps, step_index):
    return pl.BlockSpec((w_out.shape[0] // n_steps, w_out.shape[1]),
                        lambda *idx: (step_index(*idx), 0))


def _proj_kernel(x_ref, w_ref, *refs, n_scaled_tiles, scale, w_transposed, with_w_out):
    if with_w_out:
        wo_ref, o_ref, wob_ref, wb_ref = refs
        _round_slab(wo_ref, wob_ref)
    else:
        o_ref, wb_ref = refs

    @pl.when(pl.program_id(1) == 0)
    def _():
        wb_ref[...] = w_ref[...].astype(wb_ref.dtype)

    if w_transposed:
        acc = _nt_dot(x_ref[...], wb_ref[...])
    else:
        acc = jnp.dot(x_ref[...], wb_ref[...], preferred_element_type=jnp.float32)
    if n_scaled_tiles:
        acc = acc * jnp.where(pl.program_id(0) < n_scaled_tiles, jnp.float32(scale),
                              jnp.float32(1.0))
    o_ref[...] = acc.astype(o_ref.dtype)


def _project(xb2d, w, col0, n, *, w_transposed, q_cols=0, scale=1.0, w_out=None):
    m, k = xb2d.shape
    tm, tn = PROJ_TM, PROJ_TN
    n_i, n_j, j0 = m // tm, n // tn, col0 // tn
    kern = functools.partial(_proj_kernel, n_scaled_tiles=q_cols // tn, scale=scale,
                             w_transposed=w_transposed, with_w_out=w_out is not None)
    if w_transposed:
        w_spec = pl.BlockSpec((tn, k), lambda j, i: (j0 + j, 0))
        wb_shape = (tn, k)
    else:
        w_spec = pl.BlockSpec((k, tn), lambda j, i: (0, j0 + j))
        wb_shape = (k, tn)
    in_specs = [pl.BlockSpec((tm, k), lambda j, i: (i, 0)), w_spec]
    out_specs = [pl.BlockSpec((tm, tn), lambda j, i: (i, j))]
    out_shape = [jax.ShapeDtypeStruct((m, n), jnp.bfloat16)]
    operands = [xb2d, w]
    if w_out is not None:
        wo_spec = _slab_spec(w_out, n_j * n_i, lambda j, i: j * n_i + i)
        in_specs.append(wo_spec)
        out_specs.append(wo_spec)
        out_shape.append(jax.ShapeDtypeStruct(w_out.shape, jnp.bfloat16))
        operands.append(w_out)
    outs = pl.pallas_call(
        kern,
        grid=(n_j, n_i),
        in_specs=in_specs,
        out_specs=out_specs,
        out_shape=out_shape,
        scratch_shapes=[pltpu.VMEM(wb_shape, jnp.bfloat16)],
        compiler_params=_params("parallel", "arbitrary"),
        name="in_proj",
    )(*operands)
    return outs if w_out is not None else outs[0]


def _split3(a):
    p1 = a.astype(jnp.bfloat16)
    r1 = a - p1.astype(jnp.float32)
    p2 = r1.astype(jnp.bfloat16)
    p3 = (r1 - p2.astype(jnp.float32)).astype(jnp.bfloat16)
    return p1, p2, p3


def _fox_q_kernel(x_ref, wq_ref, wt_ref, b_ref, wo_ref, xb_ref, q_ref, c_ref, wob_ref,
                  wqb_ref, carry_ref, *, scale):
    t = pl.program_id(1)
    ts = x_ref.shape[0]
    _round_slab(wo_ref, wob_ref)

    @pl.when((pl.program_id(0) == 0) & (t == 0))
    def _():
        wqb_ref[...] = wq_ref[...].astype(wqb_ref.dtype)

    @pl.when(t == 0)
    def _():
        carry_ref[...] = jnp.zeros_like(carry_ref)

    xb = x_ref[...].astype(jnp.bfloat16)
    xb_ref[...] = xb
    q_ref[...] = (_nt_dot(xb, wqb_ref[...]) * scale).astype(q_ref.dtype)
    f = _nt_dot(wt_ref[...].astype(jnp.bfloat16), xb)
    z = f + b_ref[...]
    logf = jnp.minimum(z, 0.0) - jnp.log1p(jnp.exp(-jnp.abs(z)))
    row = lax.broadcasted_iota(jnp.int32, (ts, ts), 0)
    col = lax.broadcasted_iota(jnp.int32, (ts, ts), 1)
    tri = (row <= col).astype(jnp.bfloat16)
    h = logf.shape[0]
    sums = jnp.dot(jnp.concatenate(_split3(logf), axis=0), tri,
                   preferred_element_type=jnp.float32)
    c = (sums[:h] + sums[h:2 * h]) + sums[2 * h:] + carry_ref[...]
    c_ref[...] = c * LOG2E
    carry_ref[...] = c[:, ts - 1:ts]


def _fox_q_pass(x, w_in_t, n_q, b_f, scale, w_out):
    bsz, s, d = x.shape
    h = b_f.shape[0]
    ts = FOX_Q_TS
    nt = s // ts
    gate_block = (w_in_t.shape[0] - h) // h
    once = dict(pipeline_mode=pl.Buffered(1))
    wo_spec = _slab_spec(w_out, bsz * nt, lambda b, t: b * nt + t)
    return pl.pallas_call(
        functools.partial(_fox_q_kernel, scale=scale),
        grid=(bsz, nt),
        in_specs=[pl.BlockSpec((None, ts, d), lambda b, t: (b, t, 0)),
                  pl.BlockSpec((n_q, d), lambda b, t: (0, 0), **once),
                  pl.BlockSpec((h, d), lambda b, t: (gate_block, 0), **once),
                  pl.BlockSpec((h, 1), lambda b, t: (0, 0), **once),
                  wo_spec],
        out_specs=[pl.BlockSpec((None, ts, d), lambda b, t: (b, t, 0)),
                   pl.BlockSpec((None, ts, n_q), lambda b, t: (b, t, 0)),
                   pl.BlockSpec((None, h, ts), lambda b, t: (b, 0, t)),
                   wo_spec],
        out_shape=[jax.ShapeDtypeStruct((bsz, s, d), jnp.bfloat16),
                   jax.ShapeDtypeStruct((bsz, s, n_q), jnp.bfloat16),
                   jax.ShapeDtypeStruct((bsz, h, s), jnp.float32),
                   jax.ShapeDtypeStruct(w_out.shape, jnp.bfloat16)],
        scratch_shapes=[pltpu.VMEM((n_q, d), jnp.bfloat16), pltpu.VMEM((h, 1), jnp.float32)],
        compiler_params=_params("arbitrary", "arbitrary"),
        name="fox_q_pass",
    )(x, w_in_t, w_in_t, b_f.reshape(h, 1), w_out)


def _causal_tile_mask(t):
    row = lax.broadcasted_iota(jnp.int32, (t, t), 0)
    col = lax.broadcasted_iota(jnp.int32, (t, t), 1)
    return col <= row


def _row_reduce(parts, combine, lane_reduce):
    tiles = [s[:, j:j + LANES] for s in parts for j in range(0, s.shape[1], LANES)]
    return lane_reduce(functools.reduce(combine, tiles), axis=-1, keepdims=True)


def _exp2_softmax_rows(parts, dtype, with_sum):
    m = _row_reduce(parts, jnp.maximum, jnp.max)
    ps = [jnp.exp2(s - m) for s in parts]
    p = jnp.concatenate([x.astype(dtype) for x in ps], axis=1)
    return (p, _row_reduce(ps, jnp.add, jnp.sum)) if with_sum else p


def _pyramid_order(heads, n):
    items = [(h, i) for i in range(n) for h in range(heads)]
    return items[0::2] + items[1::2][::-1]


def _emit_pipelined(items, scores, softmax, output):
    n = len(items)
    s = {0: scores(items[0])}
    if n > 1:
        s[1] = scores(items[1])
    a = {0: softmax(items[0], s.pop(0))}
    for t in range(n):
        if t + 2 < n:
            s[t + 2] = scores(items[t + 2])
        if t + 1 < n:
            a[t + 1] = softmax(items[t + 1], s.pop(t + 1))
        output(items[t], a.pop(t))


def _fox_attn_kernel(q_ref, k_ref, v_ref, z_ref, c_ref, o_ref, va_ref, *, tq):
    n = q_ref.shape[0] // tq
    d = HEAD_DIM
    heads = q_ref.shape[1] // d
    mask = _causal_tile_mask(tq)
    for h in range(heads):
        va_ref[h, :, :d] = v_ref[:, h * d:(h + 1) * d]
        va_ref[h, :, d:] = jnp.ones((va_ref.shape[1], d), va_ref.dtype)

    def scores(item):
        h, i = item
        lo = i * tq
        cols = slice(h * d, (h + 1) * d)
        s = _nt_dot(q_ref[lo:lo + tq, cols], k_ref[:lo + tq, cols]) - c_ref[h, :, :lo + tq]
        diag = jnp.where(mask, s[:, lo:], MASK_VALUE)
        return [s[:, :lo], diag] if i > 0 else [diag]

    def softmax(item, parts):
        return _exp2_softmax_rows(parts, va_ref.dtype, with_sum=False)

    def output(item, p):
        h, i = item
        hi = (i + 1) * tq
        pv = jnp.dot(p, va_ref[h, :hi, :], preferred_element_type=jnp.float32)
        cols = slice(h * d, (h + 1) * d)
        gate = _silu(z_ref[hi - tq:hi, cols].astype(jnp.float32))
        o_ref[hi - tq:hi, cols] = (pv[:, :d] / pv[:, d:] * gate).astype(o_ref.dtype)

    _emit_pipelined(_pyramid_order(heads, n), scores, softmax, output)


def _fox_attention(q, kvz, c):
    bsz, s, _ = q.shape
    h, d, hs = FOX_HEADS, HEAD_DIM, FOX_HEADS_PER_STEP
    steps = h // hs
    w = hs * d
    kern = functools.partial(_fox_attn_kernel, tq=ATTN_TQ)
    return pl.pallas_call(
        kern,
        grid=(bsz, steps),
        in_specs=[pl.BlockSpec((None, s, w), lambda b, g: (b, 0, g)),
                  pl.BlockSpec((None, s, w), lambda b, g: (b, 0, g)),
                  pl.BlockSpec((None, s, w), lambda b, g: (b, 0, steps + g)),
                  pl.BlockSpec((None, s, w), lambda b, g: (b, 0, 2 * steps + g)),
                  pl.BlockSpec((None, hs, 1, s), lambda b, g: (b, g, 0, 0))],
        out_specs=pl.BlockSpec((None, s, w), lambda b, g: (b, 0, g)),
        out_shape=jax.ShapeDtypeStruct((bsz, s, h * d), jnp.bfloat16),
        scratch_shapes=[pltpu.VMEM((hs, s, 2 * d), jnp.bfloat16)],
        compiler_params=_params("parallel", "parallel"),
        name="fox_attn",
    )(q, kvz, kvz, kvz, c)


def _bias_kernel(tab_ref, o_ref):
    h = pl.program_id(0)
    t = o_ref.shape[1]
    u = lax.broadcasted_iota(jnp.int32, (8, 2 * t), 1)
    n = jnp.where(u <= t, t - u, 3 * t - u)
    max_exact = NUM_BUCKETS // 2
    large = max_exact + (jnp.log(jnp.maximum(n, 1).astype(jnp.float32) / max_exact)
                         / math.log(MAX_DISTANCE / max_exact)
                         * (NUM_BUCKETS - max_exact)).astype(jnp.int32)
    large = jnp.minimum(large, NUM_BUCKETS - 1)
    bucket = jnp.where(n < max_exact, n, large)
    val = jnp.zeros((8, 2 * t), jnp.float32)
    for b in range(NUM_BUCKETS):
        val = jnp.where(bucket == b, tab_ref[b, h], val)
    val = (val - tab_ref[NUM_BUCKETS - 1, h]) * LOG2E
    rows = jnp.concatenate([val] * (t // 8), axis=0)
    spread = pltpu.roll(rows, 0, 1, stride=1, stride_axis=0)
    o_ref[1] = spread[:, :t]
    o_ref[0] = jnp.where(_causal_tile_mask(t), spread[:, t:], MASK_VALUE)


def _bias_tiles(rel_bias, t):
    nh = rel_bias.shape[1]
    return pl.pallas_call(
        _bias_kernel,
        grid=(nh,),
        in_specs=[pl.BlockSpec(memory_space=pltpu.SMEM)],
        out_specs=pl.BlockSpec((None, 2, t, t), lambda h: (h, 0, 0, 0)),
        out_shape=jax.ShapeDtypeStruct((nh, 2, t, t), jnp.float32),
        compiler_params=_params("parallel"),
        name="bias_tiles",
    )(rel_bias)


def _diff_attn_kernel(q_ref, k_ref, v_ref, bias_ref, lq_ref, lk_ref, g_ref, o_ref,
                      *, tq, lambda_init):
    n = q_ref.shape[0] // tq
    d = HEAD_DIM
    lq = lq_ref[...]
    lk = lk_ref[...]
    dots = jnp.sum(lq * lk, axis=-1, keepdims=True)
    lam = jnp.exp(dots[0:1]) - jnp.exp(dots[1:2]) + lambda_init

    def scores(i):
        lo = i * tq
        maps = []
        for idx in range(2):
            cols = slice(idx * d, (idx + 1) * d)
            s = _nt_dot(q_ref[lo:lo + tq, cols], k_ref[:lo + tq, cols])
            parts = [s[:, :lo - tq]] if i > 1 else []
            if i > 0:
                parts.append(s[:, lo - tq:lo] + bias_ref[1])
            parts.append(s[:, lo:] + bias_ref[0])
            maps.append(parts)
        return maps

    def softmax(i, maps):
        p1, l1 = _exp2_softmax_rows(maps[0], v_ref.dtype, with_sum=True)
        p2, l2 = _exp2_softmax_rows(maps[1], v_ref.dtype, with_sum=True)
        return jnp.concatenate([p1, p2], axis=0), 1.0 / l1, lam / l2

    def output(i, args):
        p, r1, r2 = args
        hi = (i + 1) * tq
        pv = jnp.dot(p, v_ref[:hi, :], preferred_element_type=jnp.float32)
        o = pv[:tq] * r1 - pv[tq:] * r2
        o = o * lax.rsqrt(jnp.mean(o * o, axis=-1, keepdims=True) + RMS_EPS) * g_ref[...]
        o_ref[hi - tq:hi, :] = (o * (1.0 - lambda_init)).astype(o_ref.dtype)

    _emit_pipelined(list(range(n)), scores, softmax, output)


def _diff_attention(proj, bias, lam_q, lam_k, subln_g, lambda_init):
    bsz, s, _ = proj.shape
    t = ATTN_TQ
    h, d2 = DIFF_HEADS, 2 * HEAD_DIM
    kern = functools.partial(_diff_attn_kernel, tq=t, lambda_init=lambda_init)
    return pl.pallas_call(
        kern,
        grid=(bsz, h),
        in_specs=[pl.BlockSpec((None, s, d2), lambda b, hh: (b, 0, hh)),
                  pl.BlockSpec((None, s, d2), lambda b, hh: (b, 0, h + hh)),
                  pl.BlockSpec((None, s, d2), lambda b, hh: (b, 0, 2 * h + hh)),
                  pl.BlockSpec((None, 2, t, t), lambda b, hh: (hh, 0, 0, 0)),
                  pl.BlockSpec((2, HEAD_DIM), lambda b, hh: (0, 0)),
                  pl.BlockSpec((2, HEAD_DIM), lambda b, hh: (0, 0)),
                  pl.BlockSpec((1, d2), lambda b, hh: (0, 0))],
        out_specs=pl.BlockSpec((None, s, d2), lambda b, hh: (b, 0, hh)),
        out_shape=jax.ShapeDtypeStruct((bsz, s, h * d2), jnp.bfloat16),
        compiler_params=_params("parallel", "parallel"),
        name="diff_attn",
    )(proj, proj, proj, bias, lam_q, lam_k, subln_g.reshape(1, d2))


def _out_ln_kernel(*refs, alpha, gate_here):
    if gate_here:
        o_ref, z_ref, x_ref, w_ref, g_ref, b_ref, y_ref, *maybe_yb_ref = refs
        gated = o_ref[...].astype(jnp.float32) * _silu(z_ref[...].astype(jnp.float32))
        gated = gated.astype(w_ref.dtype)
    else:
        o_ref, x_ref, w_ref, g_ref, b_ref, y_ref, *maybe_yb_ref = refs
        gated = o_ref[...]
    d = w_ref.shape[1]
    cw = d // OUT_N_CHUNKS
    chunks = [slice(c, c + cw) for c in range(0, d, cw)]
    vs = [alpha * x_ref[:, c] + jnp.dot(gated, w_ref[:, c], preferred_element_type=jnp.float32)
          for c in chunks]
    mu = sum(jnp.sum(v, axis=-1, keepdims=True) for v in vs) / d
    vcs = [v - mu for v in vs]
    var = sum(jnp.sum(vc * vc, axis=-1, keepdims=True) for vc in vcs) / d
    rs = lax.rsqrt(var + LN_EPS)
    for c, vc in zip(chunks, vcs):
        y = vc * rs * g_ref[:, c] + b_ref[:, c]
        y_ref[:, c] = y
        for yb_ref in maybe_yb_ref:
            yb_ref[:, c] = y.astype(yb_ref.dtype)


def _out_ln(o2d, z_src, x2d, w_bf16, g, b, alpha, emit_bf16):
    m, d = x2d.shape
    br = o2d.shape[1]
    tm = OUT_TM
    kern = functools.partial(_out_ln_kernel, alpha=alpha, gate_here=z_src is not None)
    resident = dict(pipeline_mode=pl.Buffered(1))
    in_specs = [pl.BlockSpec((tm, br), lambda i: (i, 0))]
    operands = [o2d]
    if z_src is not None:
        z_block = z_src.shape[1] // br - 1
        in_specs.append(pl.BlockSpec((tm, br), lambda i: (i, z_block)))
        operands.append(z_src)
    in_specs += [pl.BlockSpec((tm, d), lambda i: (i, 0)),
                 pl.BlockSpec((br, d), lambda i: (0, 0), **resident),
                 pl.BlockSpec((1, d), lambda i: (0, 0), **resident),
                 pl.BlockSpec((1, d), lambda i: (0, 0), **resident)]
    operands += [x2d, w_bf16, g.reshape(1, d), b.reshape(1, d)]
    out_specs = [pl.BlockSpec((tm, d), lambda i: (i, 0))]
    out_shape = [jax.ShapeDtypeStruct((m, d), jnp.float32)]
    if emit_bf16:
        out_specs.append(pl.BlockSpec((tm, d), lambda i: (i, 0)))
        out_shape.append(jax.ShapeDtypeStruct((m, d), jnp.bfloat16))
    return pl.pallas_call(
        kern,
        grid=(m // tm,),
        in_specs=in_specs,
        out_specs=out_specs,
        out_shape=out_shape,
        compiler_params=_params("parallel"),
        name="out_ln",
    )(*operands)


def kernel(x, fox_w_in, fox_b_f, fox_w_out, diff_w_in, diff_lam_q, diff_lam_k,
           diff_subln_g, diff_w_out, rel_bias, ln_g, ln_b):
    bsz, s, d = x.shape
    depth = ln_g.shape[0]
    branch = fox_w_out.shape[1]
    alpha = (2 * depth) ** 0.25
    scale = HEAD_DIM ** -0.5 * LOG2E
    m = bsz * s

    x2d = x.reshape(m, d)
    xb2d = None
    for i in range(depth):
        j = i // 2
        if i % 2 == 0:
            w_in_t = fox_w_in[j].T
            xb, q, c, w_out_b = _fox_q_pass(x2d.reshape(bsz, s, d), w_in_t, branch, fox_b_f[j],
                                            scale, fox_w_out[j])
            proj = _project(xb.reshape(m, d), w_in_t, branch, 3 * branch, w_transposed=True)
            o = _fox_attention(q, proj.reshape(bsz, s, 3 * branch),
                               c.reshape(bsz, FOX_HEADS, 1, s))
        else:
            lambda_init = 0.8 - 0.6 * math.exp(-0.3 * i)
            proj, w_out_b = _project(xb2d, diff_w_in[j], 0, 4 * branch, w_transposed=False,
                                     q_cols=branch, scale=scale, w_out=diff_w_out[j])
            bias = _bias_tiles(rel_bias, ATTN_TQ)
            o = _diff_attention(proj.reshape(bsz, s, 4 * branch), bias, diff_lam_q[j],
                                diff_lam_k[j], diff_subln_g[j], lambda_init)
        next_is_diff = i + 1 < depth and (i + 1) % 2 == 1
        outs = _out_ln(o.reshape(m, branch), None if i % 2 == 0 else proj, x2d, w_out_b,
                       ln_g[i], ln_b[i], alpha, emit_bf16=next_is_diff)
        x2d = outs[0]
        xb2d = outs[1] if next_is_diff else None
    return x2d.reshape(bsz, s, d)
```

```python
import functools
import math

import jax
import jax.numpy as jnp
from jax import lax
from jax.experimental import pallas as pl
from jax.experimental.pallas import tpu as pltpu

LANES = 128
HEAD_DIM = 128
FOX_HEADS = 16
DIFF_HEADS = 8
NUM_BUCKETS = 32
MAX_DISTANCE = 128
LN_EPS = 1e-5
RMS_EPS = 1e-5
MASK_VALUE = -1e30
LOG2E = math.log2(math.e)

V7X_VMEM_LIMIT = 56 * 1024 * 1024

PROJ_TM = 2048
PROJ_TN = 1024
FOX_Q_TS = 512
ATTN_TQ = 256
FOX_HEADS_PER_STEP = 4
OUT_TM = 512
OUT_N_CHUNKS = 4

_NT = (((1,), (1,)), ((), ()))


def _nt_dot(a, b):
    return lax.dot_general(a, b, _NT, preferred_element_type=jnp.float32)


def _silu(z):
    return z / (1.0 + jnp.exp(-z))


def _params(*sem):
    return pltpu.CompilerParams(dimension_semantics=sem, vmem_limit_bytes=V7X_VMEM_LIMIT)


def _round_slab(wo_ref, wob_ref):
    wob_ref[...] = wo_ref[...].astype(wob_ref.dtype)


def _slab_spec(w_out, n_steps, step_index):
    assert w_out.shape[0] % (16 * n_steps) == 0
    return pl.BlockSpec((w_out.shape[0] // n_steps, w_out.shape[1]),
                        lambda *idx: (step_index(*idx), 0))


def _proj_kernel(x_ref, w_ref, *refs, n_scaled_tiles, scale, w_transposed, with_w_out):
    if with_w_out:
        wo_ref, o_ref, wob_ref, wb_ref = refs
        _round_slab(wo_ref, wob_ref)
    else:
        o_ref, wb_ref = refs

    @pl.when(pl.program_id(1) == 0)
    def _():
        wb_ref[...] = w_ref[...].astype(wb_ref.dtype)

    if w_transposed:
        acc = _nt_dot(x_ref[...], wb_ref[...])
    else:
        acc = jnp.dot(x_ref[...], wb_ref[...], preferred_element_type=jnp.float32)
    if n_scaled_tiles:
        acc = acc * jnp.where(pl.program_id(0) < n_scaled_tiles, jnp.float32(scale),
                              jnp.float32(1.0))
    o_ref[...] = acc.astype(o_ref.dtype)


def _project(xb2d, w, col0, n, *, w_transposed, q_cols=0, scale=1.0, w_out=None):
    m, k = xb2d.shape
    tm, tn = PROJ_TM, PROJ_TN
    n_i, n_j, j0 = m // tm, n // tn, col0 // tn
    kern = functools.partial(_proj_kernel, n_scaled_tiles=q_cols // tn, scale=scale,
                             w_transposed=w_transposed, with_w_out=w_out is not None)
    if w_transposed:
        w_spec = pl.BlockSpec((tn, k), lambda j, i: (j0 + j, 0))
        wb_shape = (tn, k)
    else:
        w_spec = pl.BlockSpec((k, tn), lambda j, i: (0, j0 + j))
        wb_shape = (k, tn)
    in_specs = [pl.BlockSpec((tm, k), lambda j, i: (i, 0)), w_spec]
    out_specs = [pl.BlockSpec((tm, tn), lambda j, i: (i, j))]
    out_shape = [jax.ShapeDtypeStruct((m, n), jnp.bfloat16)]
    operands = [xb2d, w]
    if w_out is not None:
        wo_spec = _slab_spec(w_out, n_j * n_i, lambda j, i: j * n_i + i)
        in_specs.append(wo_spec)
        out_specs.append(wo_spec)
        out_shape.append(jax.ShapeDtypeStruct(w_out.shape, jnp.bfloat16))
        operands.append(w_out)
    outs = pl.pallas_call(
        kern,
        grid=(n_j, n_i),
        in_specs=in_specs,
        out_specs=out_specs,
        out_shape=out_shape,
        scratch_shapes=[pltpu.VMEM(wb_shape, jnp.bfloat16)],
        compiler_params=_params("parallel", "arbitrary"),
        name="in_proj",
    )(*operands)
    return outs if w_out is not None else outs[0]


def _split3(a):
    p1 = a.astype(jnp.bfloat16)
    r1 = a - p1.astype(jnp.float32)
    p2 = r1.astype(jnp.bfloat16)
    p3 = (r1 - p2.astype(jnp.float32)).astype(jnp.bfloat16)
    return p1, p2, p3


def _fox_q_kernel(x_ref, wq_ref, wt_ref, b_ref, wo_ref, xb_ref, q_ref, c_ref, wob_ref,
                  wqb_ref, carry_ref, *, scale):
    t = pl.program_id(1)
    ts = x_ref.shape[0]
    _round_slab(wo_ref, wob_ref)

    @pl.when((pl.program_id(0) == 0) & (t == 0))
    def _():
        wqb_ref[...] = wq_ref[...].astype(wqb_ref.dtype)

    @pl.when(t == 0)
    def _():
        carry_ref[...] = jnp.zeros_like(carry_ref)

    xb = x_ref[...].astype(jnp.bfloat16)
    xb_ref[...] = xb
    q_ref[...] = (_nt_dot(xb, wqb_ref[...]) * scale).astype(q_ref.dtype)
    f = _nt_dot(wt_ref[...].astype(jnp.bfloat16), xb)
    z = f + b_ref[...]
    logf = jnp.minimum(z, 0.0) - jnp.log1p(jnp.exp(-jnp.abs(z)))
    row = lax.broadcasted_iota(jnp.int32, (ts, ts), 0)
    col = lax.broadcasted_iota(jnp.int32, (ts, ts), 1)
    tri = (row <= col).astype(jnp.bfloat16)
    h = logf.shape[0]
    sums = jnp.dot(jnp.concatenate(_split3(logf), axis=0), tri,
                   preferred_element_type=jnp.float32)
    c = (sums[:h] + sums[h:2 * h]) + sums[2 * h:] + carry_ref[...]
    c_ref[...] = c * LOG2E
    carry_ref[...] = c[:, ts - 1:ts]


def _fox_q_pass(x, w_in_t, n_q, b_f, scale, w_out):
    bsz, s, d = x.shape
    h = b_f.shape[0]
    ts = FOX_Q_TS
    nt = s // ts
    gate_block = (w_in_t.shape[0] - h) // h
    once = dict(pipeline_mode=pl.Buffered(1))
    wo_spec = _slab_spec(w_out, bsz * nt, lambda b, t: b * nt + t)
    return pl.pallas_call(
        functools.partial(_fox_q_kernel, scale=scale),
        grid=(bsz, nt),
        in_specs=[pl.BlockSpec((None, ts, d), lambda b, t: (b, t, 0)),
                  pl.BlockSpec((n_q, d), lambda b, t: (0, 0), **once),
                  pl.BlockSpec((h, d), lambda b, t: (gate_block, 0), **once),
                  pl.BlockSpec((h, 1), lambda b, t: (0, 0), **once),
                  wo_spec],
        out_specs=[pl.BlockSpec((None, ts, d), lambda b, t: (b, t, 0)),
                   pl.BlockSpec((None, ts, n_q), lambda b, t: (b, t, 0)),
                   pl.BlockSpec((None, h, ts), lambda b, t: (b, 0, t)),
                   wo_spec],
        out_shape=[jax.ShapeDtypeStruct((bsz, s, d), jnp.bfloat16),
                   jax.ShapeDtypeStruct((bsz, s, n_q), jnp.bfloat16),
                   jax.ShapeDtypeStruct((bsz, h, s), jnp.float32),
                   jax.ShapeDtypeStruct(w_out.shape, jnp.bfloat16)],
        scratch_shapes=[pltpu.VMEM((n_q, d), jnp.bfloat16), pltpu.VMEM((h, 1), jnp.float32)],
        compiler_params=_params("arbitrary", "arbitrary"),
        name="fox_q_pass",
    )(x, w_in_t, w_in_t, b_f.reshape(h, 1), w_out)


def _causal_tile_mask(t):
    row = lax.broadcasted_iota(jnp.int32, (t, t), 0)
    col = lax.broadcasted_iota(jnp.int32, (t, t), 1)
    return col <= row


def _row_reduce(parts, combine, lane_reduce):
    tiles = [s[:, j:j + LANES] for s in parts for j in range(0, s.shape[1], LANES)]
    return lane_reduce(functools.reduce(combine, tiles), axis=-1, keepdims=True)


def _exp2_softmax_rows(parts, dtype, with_sum):
    m = _row_reduce(parts, jnp.maximum, jnp.max)
    ps = [jnp.exp2(s - m) for s in parts]
    p = jnp.concatenate([x.astype(dtype) for x in ps], axis=1)
    return (p, _row_reduce(ps, jnp.add, jnp.sum)) if with_sum else p


def _pyramid_order(heads, n):
    items = [(h, i) for i in range(n) for h in range(heads)]
    return items[0::2] + items[1::2][::-1]


def _emit_pipelined(items, scores, softmax, output):
    n = len(items)
    s = {0: scores(items[0])}
    if n > 1:
        s[1] = scores(items[1])
    a = {0: softmax(items[0], s.pop(0))}
    for t in range(n):
        if t + 2 < n:
            s[t + 2] = scores(items[t + 2])
        if t + 1 < n:
            a[t + 1] = softmax(items[t + 1], s.pop(t + 1))
        output(items[t], a.pop(t))


def _fox_attn_kernel(q_ref, k_ref, v_ref, z_ref, c_ref, o_ref, va_ref, *, tq):
    n = q_ref.shape[0] // tq
    d = HEAD_DIM
    heads = q_ref.shape[1] // d
    mask = _causal_tile_mask(tq)
    for h in range(heads):
        va_ref[h, :, :d] = v_ref[:, h * d:(h + 1) * d]
        va_ref[h, :, d:] = jnp.ones((va_ref.shape[1], d), va_ref.dtype)

    def scores(item):
        h, i = item
        lo = i * tq
        cols = slice(h * d, (h + 1) * d)
        s = _nt_dot(q_ref[lo:lo + tq, cols], k_ref[:lo + tq, cols]) - c_ref[h, :, :lo + tq]
        diag = jnp.where(mask, s[:, lo:], MASK_VALUE)
        return [s[:, :lo], diag] if i > 0 else [diag]

    def softmax(item, parts):
        return _exp2_softmax_rows(parts, va_ref.dtype, with_sum=False)

    def output(item, p):
        h, i = item
        hi = (i + 1) * tq
        pv = jnp.dot(p, va_ref[h, :hi, :], preferred_element_type=jnp.float32)
        cols = slice(h * d, (h + 1) * d)
        gate = _silu(z_ref[hi - tq:hi, cols].astype(jnp.float32))
        o_ref[hi - tq:hi, cols] = (pv[:, :d] / pv[:, d:] * gate).astype(o_ref.dtype)

    _emit_pipelined(_pyramid_order(heads, n), scores, softmax, output)


def _fox_attention(q, kvz, c):
    bsz, s, _ = q.shape
    h, d, hs = FOX_HEADS, HEAD_DIM, FOX_HEADS_PER_STEP
    steps = h // hs
    w = hs * d
    kern = functools.partial(_fox_attn_kernel, tq=ATTN_TQ)
    return pl.pallas_call(
        kern,
        grid=(bsz, steps),
        in_specs=[pl.BlockSpec((None, s, w), lambda b, g: (b, 0, g)),
                  pl.BlockSpec((None, s, w), lambda b, g: (b, 0, g)),
                  pl.BlockSpec((None, s, w), lambda b, g: (b, 0, steps + g)),
                  pl.BlockSpec((None, s, w), lambda b, g: (b, 0, 2 * steps + g)),
                  pl.BlockSpec((None, hs, 1, s), lambda b, g: (b, g, 0, 0))],
        out_specs=pl.BlockSpec((None, s, w), lambda b, g: (b, 0, g)),
        out_shape=jax.ShapeDtypeStruct((bsz, s, h * d), jnp.bfloat16),
        scratch_shapes=[pltpu.VMEM((hs, s, 2 * d), jnp.bfloat16)],
        compiler_params=_params("parallel", "parallel"),
        name="fox_attn",
    )(q, kvz, kvz, kvz, c)


def _bias_kernel(tab_ref, o_ref):
    h = pl.program_id(0)
    t = o_ref.shape[1]
    u = lax.broadcasted_iota(jnp.int32, (8, 2 * t), 1)
    n = jnp.where(u <= t, t - u, 3 * t - u)
    max_exact = NUM_BUCKETS // 2
    large = max_exact + (jnp.log(jnp.maximum(n, 1).astype(jnp.float32) / max_exact)
                         / math.log(MAX_DISTANCE / max_exact)
                         * (NUM_BUCKETS - max_exact)).astype(jnp.int32)
    large = jnp.minimum(large, NUM_BUCKETS - 1)
    bucket = jnp.where(n < max_exact, n, large)
    val = jnp.zeros((8, 2 * t), jnp.float32)
    for b in range(NUM_BUCKETS):
        val = jnp.where(bucket == b, tab_ref[b, h], val)
    val = (val - tab_ref[NUM_BUCKETS - 1, h]) * LOG2E
    rows = jnp.concatenate([val] * (t // 8), axis=0)
    spread = pltpu.roll(rows, 0, 1, stride=1, stride_axis=0)
    o_ref[1] = spread[:, :t]
    o_ref[0] = jnp.where(_causal_tile_mask(t), spread[:, t:], MASK_VALUE)


def _bias_tiles(rel_bias, t):
    nh = rel_bias.shape[1]
    return pl.pallas_call(
        _bias_kernel,
        grid=(nh,),
        in_specs=[pl.BlockSpec(memory_space=pltpu.SMEM)],
        out_specs=pl.BlockSpec((None, 2, t, t), lambda h: (h, 0, 0, 0)),
        out_shape=jax.ShapeDtypeStruct((nh, 2, t, t), jnp.float32),
        compiler_params=_params("parallel"),
        name="bias_tiles",
    )(rel_bias)


def _diff_attn_kernel(q_ref, k_ref, v_ref, bias_ref, lq_ref, lk_ref, g_ref, o_ref,
                      *, tq, lambda_init):
    n = q_ref.shape[0] // tq
    d = HEAD_DIM
    lq = lq_ref[...]
    lk = lk_ref[...]
    dots = jnp.sum(lq * lk, axis=-1, keepdims=True)
    lam = jnp.exp(dots[0:1]) - jnp.exp(dots[1:2]) + lambda_init

    def scores(i):
        lo = i * tq
        maps = []
        for idx in range(2):
            cols = slice(idx * d, (idx + 1) * d)
            s = _nt_dot(q_ref[lo:lo + tq, cols], k_ref[:lo + tq, cols])
            parts = [s[:, :lo - tq]] if i > 1 else []
            if i > 0:
                parts.append(s[:, lo - tq:lo] + bias_ref[1])
            parts.append(s[:, lo:] + bias_ref[0])
            maps.append(parts)
        return maps

    def softmax(i, maps):
        p1, l1 = _exp2_softmax_rows(maps[0], v_ref.dtype, with_sum=True)
        p2, l2 = _exp2_softmax_rows(maps[1], v_ref.dtype, with_sum=True)
        return jnp.concatenate([p1, p2], axis=0), 1.0 / l1, lam / l2

    def output(i, args):
        p, r1, r2 = args
        hi = (i + 1) * tq
        pv = jnp.dot(p, v_ref[:hi, :], preferred_element_type=jnp.float32)
        o = pv[:tq] * r1 - pv[tq:] * r2
        o = o * lax.rsqrt(jnp.mean(o * o, axis=-1, keepdims=True) + RMS_EPS) * g_ref[...]
        o_ref[hi - tq:hi, :] = (o * (1.0 - lambda_init)).astype(o_ref.dtype)

    _emit_pipelined(list(range(n)), scores, softmax, output)


def _diff_attention(proj, bias, lam_q, lam_k, subln_g, lambda_init):
    bsz, s, _ = proj.shape
    t = ATTN_TQ
    h, d2 = DIFF_HEADS, 2 * HEAD_DIM
    kern = functools.partial(_diff_attn_kernel, tq=t, lambda_init=lambda_init)
    return pl.pallas_call(
        kern,
        grid=(bsz, h),
        in_specs=[pl.BlockSpec((None, s, d2), lambda b, hh: (b, 0, hh)),
                  pl.BlockSpec((None, s, d2), lambda b, hh: (b, 0, h + hh)),
                  pl.BlockSpec((None, s, d2), lambda b, hh: (b, 0, 2 * h + hh)),
                  pl.BlockSpec((None, 2, t, t), lambda b, hh: (hh, 0, 0, 0)),
                  pl.BlockSpec((2, HEAD_DIM), lambda b, hh: (0, 0)),
                  pl.BlockSpec((2, HEAD_DIM), lambda b, hh: (0, 0)),
                  pl.BlockSpec((1, d2), lambda b, hh: (0, 0))],
        out_specs=pl.BlockSpec((None, s, d2), lambda b, hh: (b, 0, hh)),
        out_shape=jax.ShapeDtypeStruct((bsz, s, h * d2), jnp.bfloat16),
        compiler_params=_params("parallel", "parallel"),
        name="diff_attn",
    )(proj, proj, proj, bias, lam_q, lam_k, subln_g.reshape(1, d2))


def _out_ln_kernel(*refs, alpha, gate_here):
    if gate_here:
        o_ref, z_ref, x_ref, w_ref, g_ref, b_ref, y_ref, *maybe_yb_ref = refs
        gated = o_ref[...].astype(jnp.float32) * _silu(z_ref[...].astype(jnp.float32))
        gated = gated.astype(w_ref.dtype)
    else:
        o_ref, x_ref, w_ref, g_ref, b_ref, y_ref, *maybe_yb_ref = refs
        gated = o_ref[...]
    d = w_ref.shape[1]
    cw = d // OUT_N_CHUNKS
    chunks = [slice(c, c + cw) for c in range(0, d, cw)]
    vs = [alpha * x_ref[:, c] + jnp.dot(gated, w_ref[:, c], preferred_element_type=jnp.float32)
          for c in chunks]
    mu = sum(jnp.sum(v, axis=-1, keepdims=True) for v in vs) / d
    vcs = [v - mu for v in vs]
    var = sum(jnp.sum(vc * vc, axis=-1, keepdims=True) for vc in vcs) / d
    rs = lax.rsqrt(var + LN_EPS)
    for c, vc in zip(chunks, vcs):
        y = vc * rs * g_ref[:, c] + b_ref[:, c]
        y_ref[:, c] = y
        for yb_ref in maybe_yb_ref:
            yb_ref[:, c] = y.astype(yb_ref.dtype)


def _out_ln(o2d, z_src, x2d, w_bf16, g, b, alpha, emit_bf16):
    m, d = x2d.shape
    br = o2d.shape[1]
    tm = OUT_TM
    kern = functools.partial(_out_ln_kernel, alpha=alpha, gate_here=z_src is not None)
    resident = dict(pipeline_mode=pl.Buffered(1))
    in_specs = [pl.BlockSpec((tm, br), lambda i: (i, 0))]
    operands = [o2d]
    if z_src is not None:
        z_block = z_src.shape[1] // br - 1
        in_specs.append(pl.BlockSpec((tm, br), lambda i: (i, z_block)))
        operands.append(z_src)
    in_specs += [pl.BlockSpec((tm, d), lambda i: (i, 0)),
                 pl.BlockSpec((br, d), lambda i: (0, 0), **resident),
                 pl.BlockSpec((1, d), lambda i: (0, 0), **resident),
                 pl.BlockSpec((1, d), lambda i: (0, 0), **resident)]
    operands += [x2d, w_bf16, g.reshape(1, d), b.reshape(1, d)]
    out_specs = [pl.BlockSpec((tm, d), lambda i: (i, 0))]
    out_shape = [jax.ShapeDtypeStruct((m, d), jnp.float32)]
    if emit_bf16:
        out_specs.append(pl.BlockSpec((tm, d), lambda i: (i, 0)))
        out_shape.append(jax.ShapeDtypeStruct((m, d), jnp.bfloat16))
    return pl.pallas_call(
        kern,
        grid=(m // tm,),
        in_specs=in_specs,
        out_specs=out_specs,
        out_shape=out_shape,
        compiler_params=_params("parallel"),
        name="out_ln",
    )(*operands)


def kernel(x, fox_w_in, fox_b_f, fox_w_out, diff_w_in, diff_lam_q, diff_lam_k,
           diff_subln_g, diff_w_out, rel_bias, ln_g, ln_b):
    bsz, s, d = x.shape
    depth = ln_g.shape[0]
    branch = fox_w_out.shape[1]
    alpha = (2 * depth) ** 0.25
    scale = HEAD_DIM ** -0.5 * LOG2E
    m = bsz * s
    assert x.dtype == jnp.float32 and fox_w_out.shape[2] == d and diff_w_out.shape[2] == d
    assert branch == FOX_HEADS * HEAD_DIM == DIFF_HEADS * 2 * HEAD_DIM == diff_w_out.shape[1]
    assert fox_w_in.shape[1:] == (d, 4 * branch + FOX_HEADS)
    assert diff_w_in.shape[1:] == (d, 4 * branch) and rel_bias.shape == (NUM_BUCKETS, DIFF_HEADS)
    assert s % FOX_Q_TS == 0 and s % ATTN_TQ == 0 and ATTN_TQ >= MAX_DISTANCE
    assert m % PROJ_TM == 0 and m % OUT_TM == 0 and branch % PROJ_TN == 0
    assert FOX_HEADS % FOX_HEADS_PER_STEP == 0 and d % (OUT_N_CHUNKS * LANES) == 0

    x2d = x.reshape(m, d)
    xb2d = None
    for i in range(depth):
        j = i // 2
        if i % 2 == 0:
            w_in_t = fox_w_in[j].T
            xb, q, c, w_out_b = _fox_q_pass(x2d.reshape(bsz, s, d), w_in_t, branch, fox_b_f[j],
                                            scale, fox_w_out[j])
            proj = _project(xb.reshape(m, d), w_in_t, branch, 3 * branch, w_transposed=True)
            o = _fox_attention(q, proj.reshape(bsz, s, 3 * branch),
                               c.reshape(bsz, FOX_HEADS, 1, s))
        else:
            lambda_init = 0.8 - 0.6 * math.exp(-0.3 * i)
            proj, w_out_b = _project(xb2d, diff_w_in[j], 0, 4 * branch, w_transposed=False,
                                     q_cols=branch, scale=scale, w_out=diff_w_out[j])
            bias = _bias_tiles(rel_bias, ATTN_TQ)
            o = _diff_attention(proj.reshape(bsz, s, 4 * branch), bias, diff_lam_q[j],
                                diff_lam_k[j], diff_subln_g[j], lambda_init)
        next_is_diff = i + 1 < depth and (i + 1) % 2 == 1
        outs = _out_ln(o.reshape(m, branch), None if i % 2 == 0 else proj, x2d, w_out_b,
                       ln_g[i], ln_b[i], alpha, emit_bf16=next_is_diff)
        x2d = outs[0]
        xb2d = outs[1] if next_is_diff else None
    return x2d.reshape(bsz, s, d)
```

```python
import functools
import math

import jax
import jax.numpy as jnp
from jax import lax
from jax.experimental import pallas as pl
from jax.experimental.pallas import tpu as pltpu

LANES = 128
HEAD_DIM = 128
FOX_HEADS = 16
DIFF_HEADS = 8
NUM_BUCKETS = 32
MAX_DISTANCE = 128
LN_EPS = 1e-5
RMS_EPS = 1e-5
MASK_VALUE = -1e30
LOG2E = math.log2(math.e)

V7X_VMEM_LIMIT = 56 * 1024 * 1024

PROJ_TM = 2048
PROJ_TN = 1024
FOX_Q_TS = 512
ATTN_TQ = 256
FOX_HEADS_PER_STEP = 4
DIFF_HEADS_PER_STEP = 2
OUT_TM = 512
OUT_N_CHUNKS = 4

_NT = (((1,), (1,)), ((), ()))


def _nt_dot(a, b):
    return lax.dot_general(a, b, _NT, preferred_element_type=jnp.float32)


def _silu(z):
    return z / (1.0 + jnp.exp(-z))


def _params(*sem):
    return pltpu.CompilerParams(dimension_semantics=sem, vmem_limit_bytes=V7X_VMEM_LIMIT)


def _round_slab(wo_ref, wob_ref):
    wob_ref[...] = wo_ref[...].astype(wob_ref.dtype)


def _slab_spec(w_out, n_steps, step_index):
    assert w_out.shape[0] % (16 * n_steps) == 0
    return pl.BlockSpec((w_out.shape[0] // n_steps, w_out.shape[1]),
                        lambda *idx: (step_index(*idx), 0))


def _proj_kernel(x_ref, w_ref, *refs, n_scaled_tiles, scale, w_transposed, with_w_out):
    if with_w_out:
        wo_ref, o_ref, wob_ref, wb_ref = refs
        _round_slab(wo_ref, wob_ref)
    else:
        o_ref, wb_ref = refs

    @pl.when(pl.program_id(1) == 0)
    def _():
        wb_ref[...] = w_ref[...].astype(wb_ref.dtype)

    if w_transposed:
        acc = _nt_dot(x_ref[...], wb_ref[...])
    else:
        acc = jnp.dot(x_ref[...], wb_ref[...], preferred_element_type=jnp.float32)
    if n_scaled_tiles:
        acc = acc * jnp.where(pl.program_id(0) < n_scaled_tiles, jnp.float32(scale),
                              jnp.float32(1.0))
    o_ref[...] = acc.astype(o_ref.dtype)


def _project(xb2d, w, col0, n, *, w_transposed, q_cols=0, scale=1.0, w_out=None):
    m, k = xb2d.shape
    tm, tn = PROJ_TM, PROJ_TN
    n_i, n_j, j0 = m // tm, n // tn, col0 // tn
    kern = functools.partial(_proj_kernel, n_scaled_tiles=q_cols // tn, scale=scale,
                             w_transposed=w_transposed, with_w_out=w_out is not None)
    if w_transposed:
        w_spec = pl.BlockSpec((tn, k), lambda j, i: (j0 + j, 0))
        wb_shape = (tn, k)
    else:
        w_spec = pl.BlockSpec((k, tn), lambda j, i: (0, j0 + j))
        wb_shape = (k, tn)
    in_specs = [pl.BlockSpec((tm, k), lambda j, i: (i, 0)), w_spec]
    out_specs = [pl.BlockSpec((tm, tn), lambda j, i: (i, j))]
    out_shape = [jax.ShapeDtypeStruct((m, n), jnp.bfloat16)]
    operands = [xb2d, w]
    if w_out is not None:
        wo_spec = _slab_spec(w_out, n_j * n_i, lambda j, i: j * n_i + i)
        in_specs.append(wo_spec)
        out_specs.append(wo_spec)
        out_shape.append(jax.ShapeDtypeStruct(w_out.shape, jnp.bfloat16))
        operands.append(w_out)
    outs = pl.pallas_call(
        kern,
        grid=(n_j, n_i),
        in_specs=in_specs,
        out_specs=out_specs,
        out_shape=out_shape,
        scratch_shapes=[pltpu.VMEM(wb_shape, jnp.bfloat16)],
        compiler_params=_params("parallel", "arbitrary"),
        name="in_proj",
    )(*operands)
    return outs if w_out is not None else outs[0]


def _split3(a):
    p1 = a.astype(jnp.bfloat16)
    r1 = a - p1.astype(jnp.float32)
    p2 = r1.astype(jnp.bfloat16)
    p3 = (r1 - p2.astype(jnp.float32)).astype(jnp.bfloat16)
    return p1, p2, p3


def _fox_q_kernel(x_ref, wq_ref, wt_ref, b_ref, wo_ref, xb_ref, q_ref, c_ref, wob_ref,
                  wqb_ref, carry_ref, *, scale):
    t = pl.program_id(1)
    ts = x_ref.shape[0]
    _round_slab(wo_ref, wob_ref)

    @pl.when((pl.program_id(0) == 0) & (t == 0))
    def _():
        wqb_ref[...] = wq_ref[...].astype(wqb_ref.dtype)

    @pl.when(t == 0)
    def _():
        carry_ref[...] = jnp.zeros_like(carry_ref)

    xb = x_ref[...].astype(jnp.bfloat16)
    xb_ref[...] = xb
    q_ref[...] = (_nt_dot(xb, wqb_ref[...]) * scale).astype(q_ref.dtype)
    f = _nt_dot(wt_ref[...].astype(jnp.bfloat16), xb)
    z = f + b_ref[...]
    logf = jnp.minimum(z, 0.0) - jnp.log1p(jnp.exp(-jnp.abs(z)))
    row = lax.broadcasted_iota(jnp.int32, (ts, ts), 0)
    col = lax.broadcasted_iota(jnp.int32, (ts, ts), 1)
    tri = (row <= col).astype(jnp.bfloat16)
    h = logf.shape[0]
    sums = jnp.dot(jnp.concatenate(_split3(logf), axis=0), tri,
                   preferred_element_type=jnp.float32)
    c = (sums[:h] + sums[h:2 * h]) + sums[2 * h:] + carry_ref[...]
    c_ref[...] = c * LOG2E
    carry_ref[...] = c[:, ts - 1:ts]


def _fox_q_pass(x, w_in_t, n_q, b_f, scale, w_out):
    bsz, s, d = x.shape
    h = b_f.shape[0]
    ts = FOX_Q_TS
    nt = s // ts
    gate_block = (w_in_t.shape[0] - h) // h
    once = dict(pipeline_mode=pl.Buffered(1))
    wo_spec = _slab_spec(w_out, bsz * nt, lambda b, t: b * nt + t)
    return pl.pallas_call(
        functools.partial(_fox_q_kernel, scale=scale),
        grid=(bsz, nt),
        in_specs=[pl.BlockSpec((None, ts, d), lambda b, t: (b, t, 0)),
                  pl.BlockSpec((n_q, d), lambda b, t: (0, 0), **once),
                  pl.BlockSpec((h, d), lambda b, t: (gate_block, 0), **once),
                  pl.BlockSpec((h, 1), lambda b, t: (0, 0), **once),
                  wo_spec],
        out_specs=[pl.BlockSpec((None, ts, d), lambda b, t: (b, t, 0)),
                   pl.BlockSpec((None, ts, n_q), lambda b, t: (b, t, 0)),
                   pl.BlockSpec((None, h, ts), lambda b, t: (b, 0, t)),
                   wo_spec],
        out_shape=[jax.ShapeDtypeStruct((bsz, s, d), jnp.bfloat16),
                   jax.ShapeDtypeStruct((bsz, s, n_q), jnp.bfloat16),
                   jax.ShapeDtypeStruct((bsz, h, s), jnp.float32),
                   jax.ShapeDtypeStruct(w_out.shape, jnp.bfloat16)],
        scratch_shapes=[pltpu.VMEM((n_q, d), jnp.bfloat16), pltpu.VMEM((h, 1), jnp.float32)],
        compiler_params=_params("arbitrary", "arbitrary"),
        name="fox_q_pass",
    )(x, w_in_t, w_in_t, b_f.reshape(h, 1), w_out)


def _causal_tile_mask(t):
    row = lax.broadcasted_iota(jnp.int32, (t, t), 0)
    col = lax.broadcasted_iota(jnp.int32, (t, t), 1)
    return col <= row


def _row_reduce(parts, combine, lane_reduce):
    tiles = [s[:, j:j + LANES] for s in parts for j in range(0, s.shape[1], LANES)]
    return lane_reduce(functools.reduce(combine, tiles), axis=-1, keepdims=True)


def _exp2_softmax_rows(parts, dtype, with_sum):
    m = _row_reduce(parts, jnp.maximum, jnp.max)
    ps = [jnp.exp2(s - m) for s in parts]
    p = jnp.concatenate([x.astype(dtype) for x in ps], axis=1)
    return (p, _row_reduce(ps, jnp.add, jnp.sum)) if with_sum else p


def _pyramid_order(heads, n):
    items = [(h, i) for i in range(n) for h in range(heads)]
    return items[0::2] + items[1::2][::-1]


def _emit_pipelined(items, scores, softmax, output):
    n = len(items)
    s = {0: scores(items[0])}
    if n > 1:
        s[1] = scores(items[1])
    a = {0: softmax(items[0], s.pop(0))}
    for t in range(n):
        if t + 2 < n:
            s[t + 2] = scores(items[t + 2])
        if t + 1 < n:
            a[t + 1] = softmax(items[t + 1], s.pop(t + 1))
        output(items[t], a.pop(t))


def _fox_attn_kernel(q_ref, k_ref, v_ref, z_ref, c_ref, o_ref, va_ref, *, tq):
    n = q_ref.shape[0] // tq
    d = HEAD_DIM
    heads = q_ref.shape[1] // d
    mask = _causal_tile_mask(tq)
    for h in range(heads):
        va_ref[h, :, :d] = v_ref[:, h * d:(h + 1) * d]
        va_ref[h, :, d:] = jnp.ones((va_ref.shape[1], d), va_ref.dtype)

    def scores(item):
        h, i = item
        lo = i * tq
        cols = slice(h * d, (h + 1) * d)
        s = _nt_dot(q_ref[lo:lo + tq, cols], k_ref[:lo + tq, cols]) - c_ref[h, :, :lo + tq]
        diag = jnp.where(mask, s[:, lo:], MASK_VALUE)
        return [s[:, :lo], diag] if i > 0 else [diag]

    def softmax(item, parts):
        return _exp2_softmax_rows(parts, va_ref.dtype, with_sum=False)

    def output(item, p):
        h, i = item
        hi = (i + 1) * tq
        pv = jnp.dot(p, va_ref[h, :hi, :], preferred_element_type=jnp.float32)
        cols = slice(h * d, (h + 1) * d)
        gate = _silu(z_ref[hi - tq:hi, cols].astype(jnp.float32))
        o_ref[hi - tq:hi, cols] = (pv[:, :d] / pv[:, d:] * gate).astype(o_ref.dtype)

    _emit_pipelined(_pyramid_order(heads, n), scores, softmax, output)


def _fox_attention(q, kvz, c):
    bsz, s, _ = q.shape
    h, d, hs = FOX_HEADS, HEAD_DIM, FOX_HEADS_PER_STEP
    steps = h // hs
    w = hs * d
    kern = functools.partial(_fox_attn_kernel, tq=ATTN_TQ)
    return pl.pallas_call(
        kern,
        grid=(bsz, steps),
        in_specs=[pl.BlockSpec((None, s, w), lambda b, g: (b, 0, g)),
                  pl.BlockSpec((None, s, w), lambda b, g: (b, 0, g)),
                  pl.BlockSpec((None, s, w), lambda b, g: (b, 0, steps + g)),
                  pl.BlockSpec((None, s, w), lambda b, g: (b, 0, 2 * steps + g)),
                  pl.BlockSpec((None, hs, 1, s), lambda b, g: (b, g, 0, 0))],
        out_specs=pl.BlockSpec((None, s, w), lambda b, g: (b, 0, g)),
        out_shape=jax.ShapeDtypeStruct((bsz, s, h * d), jnp.bfloat16),
        scratch_shapes=[pltpu.VMEM((hs, s, 2 * d), jnp.bfloat16)],
        compiler_params=_params("parallel", "parallel"),
        name="fox_attn",
    )(q, kvz, kvz, kvz, c)


def _bias_kernel(tab_ref, o_ref):
    h = pl.program_id(0)
    t = o_ref.shape[1]
    u = lax.broadcasted_iota(jnp.int32, (8, 2 * t), 1)
    n = jnp.where(u <= t, t - u, 3 * t - u)
    max_exact = NUM_BUCKETS // 2
    large = max_exact + (jnp.log(jnp.maximum(n, 1).astype(jnp.float32) / max_exact)
                         / math.log(MAX_DISTANCE / max_exact)
                         * (NUM_BUCKETS - max_exact)).astype(jnp.int32)
    large = jnp.minimum(large, NUM_BUCKETS - 1)
    bucket = jnp.where(n < max_exact, n, large)
    val = jnp.zeros((8, 2 * t), jnp.float32)
    for b in range(NUM_BUCKETS):
        val = jnp.where(bucket == b, tab_ref[b, h], val)
    val = (val - tab_ref[NUM_BUCKETS - 1, h]) * LOG2E
    rows = jnp.concatenate([val] * (t // 8), axis=0)
    spread = pltpu.roll(rows, 0, 1, stride=1, stride_axis=0)
    o_ref[1] = spread[:, :t]
    o_ref[0] = jnp.where(_causal_tile_mask(t), spread[:, t:], MASK_VALUE)


def _bias_tiles(rel_bias, t):
    nh = rel_bias.shape[1]
    return pl.pallas_call(
        _bias_kernel,
        grid=(nh,),
        in_specs=[pl.BlockSpec(memory_space=pltpu.SMEM)],
        out_specs=pl.BlockSpec((None, 2, t, t), lambda h: (h, 0, 0, 0)),
        out_shape=jax.ShapeDtypeStruct((nh, 2, t, t), jnp.float32),
        compiler_params=_params("parallel"),
        name="bias_tiles",
    )(rel_bias)


def _diff_attn_kernel(q_ref, k_ref, v_ref, bias_ref, lq_ref, lk_ref, g_ref, o_ref,
                      *, tq, lambda_init):
    n = q_ref.shape[0] // tq
    d = HEAD_DIM
    heads = q_ref.shape[1] // (2 * d)
    lq = lq_ref[...]
    lk = lk_ref[...]
    dots = jnp.sum(lq * lk, axis=-1, keepdims=True)
    lam = jnp.exp(dots[0:1]) - jnp.exp(dots[1:2]) + lambda_init

    def scores(item):
        h, i = item
        lo = i * tq
        maps = []
        for idx in range(2):
            cols = slice((2 * h + idx) * d, (2 * h + idx + 1) * d)
            s = _nt_dot(q_ref[lo:lo + tq, cols], k_ref[:lo + tq, cols])
            parts = [s[:, :lo - tq]] if i > 1 else []
            if i > 0:
                parts.append(s[:, lo - tq:lo] + bias_ref[h, 1])
            parts.append(s[:, lo:] + bias_ref[h, 0])
            maps.append(parts)
        return maps

    def softmax(item, maps):
        p1, l1 = _exp2_softmax_rows(maps[0], v_ref.dtype, with_sum=True)
        p2, l2 = _exp2_softmax_rows(maps[1], v_ref.dtype, with_sum=True)
        return jnp.concatenate([p1, p2], axis=0), 1.0 / l1, lam / l2

    def output(item, args):
        h, i = item
        p, r1, r2 = args
        hi = (i + 1) * tq
        cols = slice(2 * h * d, 2 * (h + 1) * d)
        pv = jnp.dot(p, v_ref[:hi, cols], preferred_element_type=jnp.float32)
        o = pv[:tq] * r1 - pv[tq:] * r2
        o = o * lax.rsqrt(jnp.mean(o * o, axis=-1, keepdims=True) + RMS_EPS) * g_ref[...]
        o_ref[hi - tq:hi, cols] = (o * (1.0 - lambda_init)).astype(o_ref.dtype)

    _emit_pipelined([(h, i) for h in range(heads) for i in range(n)], scores, softmax, output)


def _diff_attention(proj, bias, lam_q, lam_k, subln_g, lambda_init):
    bsz, s, _ = proj.shape
    t = ATTN_TQ
    h, d2, hs = DIFF_HEADS, 2 * HEAD_DIM, DIFF_HEADS_PER_STEP
    steps = h // hs
    w = hs * d2
    kern = functools.partial(_diff_attn_kernel, tq=t, lambda_init=lambda_init)
    return pl.pallas_call(
        kern,
        grid=(bsz, steps),
        in_specs=[pl.BlockSpec((None, s, w), lambda b, g: (b, 0, g)),
                  pl.BlockSpec((None, s, w), lambda b, g: (b, 0, steps + g)),
                  pl.BlockSpec((None, s, w), lambda b, g: (b, 0, 2 * steps + g)),
                  pl.BlockSpec((hs, 2, t, t), lambda b, g: (g, 0, 0, 0)),
                  pl.BlockSpec((2, HEAD_DIM), lambda b, g: (0, 0)),
                  pl.BlockSpec((2, HEAD_DIM), lambda b, g: (0, 0)),
                  pl.BlockSpec((1, d2), lambda b, g: (0, 0))],
        out_specs=pl.BlockSpec((None, s, w), lambda b, g: (b, 0, g)),
        out_shape=jax.ShapeDtypeStruct((bsz, s, h * d2), jnp.bfloat16),
        compiler_params=_params("parallel", "parallel"),
        name="diff_attn",
    )(proj, proj, proj, bias, lam_q, lam_k, subln_g.reshape(1, d2))


def _out_ln_kernel(*refs, alpha, gate_here):
    if gate_here:
        o_ref, z_ref, x_ref, w_ref, g_ref, b_ref, y_ref, *maybe_yb_ref = refs
        gated = o_ref[...].astype(jnp.float32) * _silu(z_ref[...].astype(jnp.float32))
        gated = gated.astype(w_ref.dtype)
    else:
        o_ref, x_ref, w_ref, g_ref, b_ref, y_ref, *maybe_yb_ref = refs
        gated = o_ref[...]
    d = w_ref.shape[1]
    cw = d // OUT_N_CHUNKS
    chunks = [slice(c, c + cw) for c in range(0, d, cw)]
    vs = [alpha * x_ref[:, c] + jnp.dot(gated, w_ref[:, c], preferred_element_type=jnp.float32)
          for c in chunks]
    mu = sum(jnp.sum(v, axis=-1, keepdims=True) for v in vs) / d
    vcs = [v - mu for v in vs]
    var = sum(jnp.sum(vc * vc, axis=-1, keepdims=True) for vc in vcs) / d
    rs = lax.rsqrt(var + LN_EPS)
    for c, vc in zip(chunks, vcs):
        y = vc * rs * g_ref[:, c] + b_ref[:, c]
        y_ref[:, c] = y
        for yb_ref in maybe_yb_ref:
            yb_ref[:, c] = y.astype(yb_ref.dtype)


def _out_ln(o2d, z_src, x2d, w_bf16, g, b, alpha, emit_bf16):
    m, d = x2d.shape
    br = o2d.shape[1]
    tm = OUT_TM
    kern = functools.partial(_out_ln_kernel, alpha=alpha, gate_here=z_src is not None)
    resident = dict(pipeline_mode=pl.Buffered(1))
    in_specs = [pl.BlockSpec((tm, br), lambda i: (i, 0))]
    operands = [o2d]
    if z_src is not None:
        z_block = z_src.shape[1] // br - 1
        in_specs.append(pl.BlockSpec((tm, br), lambda i: (i, z_block)))
        operands.append(z_src)
    in_specs += [pl.BlockSpec((tm, d), lambda i: (i, 0)),
                 pl.BlockSpec((br, d), lambda i: (0, 0), **resident),
                 pl.BlockSpec((1, d), lambda i: (0, 0), **resident),
                 pl.BlockSpec((1, d), lambda i: (0, 0), **resident)]
    operands += [x2d, w_bf16, g.reshape(1, d), b.reshape(1, d)]
    out_specs = [pl.BlockSpec((tm, d), lambda i: (i, 0))]
    out_shape = [jax.ShapeDtypeStruct((m, d), jnp.float32)]
    if emit_bf16:
        out_specs.append(pl.BlockSpec((tm, d), lambda i: (i, 0)))
        out_shape.append(jax.ShapeDtypeStruct((m, d), jnp.bfloat16))
    return pl.pallas_call(
        kern,
        grid=(m // tm,),
        in_specs=in_specs,
        out_specs=out_specs,
        out_shape=out_shape,
        compiler_params=_params("parallel"),
        name="out_ln",
    )(*operands)


def kernel(x, fox_w_in, fox_b_f, fox_w_out, diff_w_in, diff_lam_q, diff_lam_k,
           diff_subln_g, diff_w_out, rel_bias, ln_g, ln_b):
    bsz, s, d = x.shape
    depth = ln_g.shape[0]
    branch = fox_w_out.shape[1]
    alpha = (2 * depth) ** 0.25
    scale = HEAD_DIM ** -0.5 * LOG2E
    m = bsz * s
    assert x.dtype == jnp.float32 and fox_w_out.shape[2] == d and diff_w_out.shape[2] == d
    assert branch == FOX_HEADS * HEAD_DIM == DIFF_HEADS * 2 * HEAD_DIM == diff_w_out.shape[1]
    assert fox_w_in.shape[1:] == (d, 4 * branch + FOX_HEADS)
    assert diff_w_in.shape[1:] == (d, 4 * branch) and rel_bias.shape == (NUM_BUCKETS, DIFF_HEADS)
    assert s % FOX_Q_TS == 0 and s % ATTN_TQ == 0 and ATTN_TQ >= MAX_DISTANCE
    assert m % PROJ_TM == 0 and m % OUT_TM == 0 and branch % PROJ_TN == 0
    assert FOX_HEADS % FOX_HEADS_PER_STEP == 0 and DIFF_HEADS % DIFF_HEADS_PER_STEP == 0
    assert d % (OUT_N_CHUNKS * LANES) == 0

    x2d = x.reshape(m, d)
    xb2d = None
    for i in range(depth):
        j = i // 2
        if i % 2 == 0:
            w_in_t = fox_w_in[j].T
            xb, q, c, w_out_b = _fox_q_pass(x2d.reshape(bsz, s, d), w_in_t, branch, fox_b_f[j],
                                            scale, fox_w_out[j])
            proj = _project(xb.reshape(m, d), w_in_t, branch, 3 * branch, w_transposed=True)
            o = _fox_attention(q, proj.reshape(bsz, s, 3 * branch),
                               c.reshape(bsz, FOX_HEADS, 1, s))
        else:
            lambda_init = 0.8 - 0.6 * math.exp(-0.3 * i)
            proj, w_out_b = _project(xb2d, diff_w_in[j], 0, 4 * branch, w_transposed=False,
                                     q_cols=branch, scale=scale, w_out=diff_w_out[j])
            bias = _bias_tiles(rel_bias, ATTN_TQ)
            o = _diff_attention(proj.reshape(bsz, s, 4 * branch), bias, diff_lam_q[j],
                                diff_lam_k[j], diff_subln_g[j], lambda_init)
        next_is_diff = i + 1 < depth and (i + 1) % 2 == 1
        outs = _out_ln(o.reshape(m, branch), None if i % 2 == 0 else proj, x2d, w_out_b,
                       ln_g[i], ln_b[i], alpha, emit_bf16=next_is_diff)
        x2d = outs[0]
        xb2d = outs[1] if next_is_diff else None
    return x2d.reshape(bsz, s, d)
```

```python
import functools
import math

import jax
import jax.numpy as jnp
from jax import lax
from jax.experimental import pallas as pl
from jax.experimental.pallas import tpu as pltpu

LANES = 128
HEAD_DIM = 128
FOX_HEADS = 16
DIFF_HEADS = 8
NUM_BUCKETS = 32
MAX_DISTANCE = 128
LN_EPS = 1e-5
RMS_EPS = 1e-5
MASK_VALUE = -1e30
LOG2E = math.log2(math.e)

V7X_VMEM_LIMIT = 56 * 1024 * 1024

PROJ_TM = 2048
PROJ_TN = 1024
FOX_Q_TS = 512
ATTN_TQ = 256
FOX_HEADS_PER_STEP = 4
DIFF_HEADS_PER_STEP = 2
OUT_TM = 512
OUT_N_CHUNKS = 4

_NT = (((1,), (1,)), ((), ()))


def _nt_dot(a, b):
    return lax.dot_general(a, b, _NT, preferred_element_type=jnp.float32)


def _silu(z):
    return z / (1.0 + jnp.exp(-z))


def _params(*sem):
    return pltpu.CompilerParams(dimension_semantics=sem, vmem_limit_bytes=V7X_VMEM_LIMIT)


def _round_slab(wo_ref, wob_ref):
    wob_ref[...] = wo_ref[...].astype(wob_ref.dtype)


def _slab_spec(w_out, n_steps, step_index):
    assert w_out.shape[0] % (16 * n_steps) == 0
    return pl.BlockSpec((w_out.shape[0] // n_steps, w_out.shape[1]),
                        lambda *idx: (step_index(*idx), 0))


def _proj_kernel(x_ref, w_ref, *refs, n_scaled_tiles, scale, w_transposed, with_w_out):
    if with_w_out:
        wo_ref, o_ref, wob_ref, wb_ref = refs
        _round_slab(wo_ref, wob_ref)
    else:
        o_ref, wb_ref = refs

    @pl.when(pl.program_id(1) == 0)
    def _():
        wb_ref[...] = w_ref[...].astype(wb_ref.dtype)

    if w_transposed:
        acc = _nt_dot(x_ref[...], wb_ref[...])
    else:
        acc = jnp.dot(x_ref[...], wb_ref[...], preferred_element_type=jnp.float32)
    if n_scaled_tiles:
        acc = acc * jnp.where(pl.program_id(0) < n_scaled_tiles, jnp.float32(scale),
                              jnp.float32(1.0))
    o_ref[...] = acc.astype(o_ref.dtype)


def _project(xb2d, w, col0, n, *, w_transposed, q_cols=0, scale=1.0, w_out=None):
    m, k = xb2d.shape
    tm, tn = PROJ_TM, PROJ_TN
    n_i, n_j, j0 = m // tm, n // tn, col0 // tn
    kern = functools.partial(_proj_kernel, n_scaled_tiles=q_cols // tn, scale=scale,
                             w_transposed=w_transposed, with_w_out=w_out is not None)
    if w_transposed:
        w_spec = pl.BlockSpec((tn, k), lambda j, i: (j0 + j, 0))
        wb_shape = (tn, k)
    else:
        w_spec = pl.BlockSpec((k, tn), lambda j, i: (0, j0 + j))
        wb_shape = (k, tn)
    in_specs = [pl.BlockSpec((tm, k), lambda j, i: (i, 0)), w_spec]
    out_specs = [pl.BlockSpec((tm, tn), lambda j, i: (i, j))]
    out_shape = [jax.ShapeDtypeStruct((m, n), jnp.bfloat16)]
    operands = [xb2d, w]
    if w_out is not None:
        wo_spec = _slab_spec(w_out, n_j * n_i, lambda j, i: j * n_i + i)
        in_specs.append(wo_spec)
        out_specs.append(wo_spec)
        out_shape.append(jax.ShapeDtypeStruct(w_out.shape, jnp.bfloat16))
        operands.append(w_out)
    outs = pl.pallas_call(
        kern,
        grid=(n_j, n_i),
        in_specs=in_specs,
        out_specs=out_specs,
        out_shape=out_shape,
        scratch_shapes=[pltpu.VMEM(wb_shape, jnp.bfloat16)],
        compiler_params=_params("parallel", "arbitrary"),
        name="in_proj",
    )(*operands)
    return outs if w_out is not None else outs[0]


def _split3(a):
    p1 = a.astype(jnp.bfloat16)
    r1 = a - p1.astype(jnp.float32)
    p2 = r1.astype(jnp.bfloat16)
    p3 = (r1 - p2.astype(jnp.float32)).astype(jnp.bfloat16)
    return p1, p2, p3


def _fox_q_kernel(x_ref, wq_ref, wt_ref, b_ref, wo_ref, xb_ref, q_ref, c_ref, wob_ref,
                  wqb_ref, carry_ref, *, scale):
    t = pl.program_id(1)
    ts = x_ref.shape[0]
    _round_slab(wo_ref, wob_ref)

    @pl.when((pl.program_id(0) == 0) & (t == 0))
    def _():
        wqb_ref[...] = wq_ref[...].astype(wqb_ref.dtype)

    @pl.when(t == 0)
    def _():
        carry_ref[...] = jnp.zeros_like(carry_ref)

    xb = x_ref[...].astype(jnp.bfloat16)
    xb_ref[...] = xb
    q_ref[...] = (_nt_dot(xb, wqb_ref[...]) * scale).astype(q_ref.dtype)
    f = _nt_dot(wt_ref[...].astype(jnp.bfloat16), xb)
    z = f + b_ref[...]
    logf = jnp.minimum(z, 0.0) - jnp.log1p(jnp.exp(-jnp.abs(z)))
    row = lax.broadcasted_iota(jnp.int32, (ts, ts), 0)
    col = lax.broadcasted_iota(jnp.int32, (ts, ts), 1)
    tri = (row <= col).astype(jnp.bfloat16)
    h = logf.shape[0]
    sums = jnp.dot(jnp.concatenate(_split3(logf), axis=0), tri,
                   preferred_element_type=jnp.float32)
    c = (sums[:h] + sums[h:2 * h]) + sums[2 * h:] + carry_ref[...]
    pieces = [p.astype(jnp.float32) for p in _split3(c * (-LOG2E))]
    pad = jnp.zeros((LANES - 3 * h, ts), jnp.float32)
    c_ref[...] = jnp.concatenate(pieces + [pad], axis=0).T.astype(c_ref.dtype)
    carry_ref[...] = c[:, ts - 1:ts]


def _fox_q_pass(x, w_in_t, n_q, b_f, scale, w_out):
    bsz, s, d = x.shape
    h = b_f.shape[0]
    ts = FOX_Q_TS
    nt = s // ts
    gate_block = (w_in_t.shape[0] - h) // h
    once = dict(pipeline_mode=pl.Buffered(1))
    wo_spec = _slab_spec(w_out, bsz * nt, lambda b, t: b * nt + t)
    return pl.pallas_call(
        functools.partial(_fox_q_kernel, scale=scale),
        grid=(bsz, nt),
        in_specs=[pl.BlockSpec((None, ts, d), lambda b, t: (b, t, 0)),
                  pl.BlockSpec((n_q, d), lambda b, t: (0, 0), **once),
                  pl.BlockSpec((h, d), lambda b, t: (gate_block, 0), **once),
                  pl.BlockSpec((h, 1), lambda b, t: (0, 0), **once),
                  wo_spec],
        out_specs=[pl.BlockSpec((None, ts, d), lambda b, t: (b, t, 0)),
                   pl.BlockSpec((None, ts, n_q), lambda b, t: (b, t, 0)),
                   pl.BlockSpec((None, ts, LANES), lambda b, t: (b, t, 0)),
                   wo_spec],
        out_shape=[jax.ShapeDtypeStruct((bsz, s, d), jnp.bfloat16),
                   jax.ShapeDtypeStruct((bsz, s, n_q), jnp.bfloat16),
                   jax.ShapeDtypeStruct((bsz, s, LANES), jnp.bfloat16),
                   jax.ShapeDtypeStruct(w_out.shape, jnp.bfloat16)],
        scratch_shapes=[pltpu.VMEM((n_q, d), jnp.bfloat16), pltpu.VMEM((h, 1), jnp.float32)],
        compiler_params=_params("arbitrary", "arbitrary"),
        name="fox_q_pass",
    )(x, w_in_t, w_in_t, b_f.reshape(h, 1), w_out)


def _causal_tile_mask(t):
    row = lax.broadcasted_iota(jnp.int32, (t, t), 0)
    col = lax.broadcasted_iota(jnp.int32, (t, t), 1)
    return col <= row


def _row_reduce(parts, combine, lane_reduce):
    tiles = [s[:, j:j + LANES] for s in parts for j in range(0, s.shape[1], LANES)]
    return lane_reduce(functools.reduce(combine, tiles), axis=-1, keepdims=True)


def _exp2_softmax_rows(parts, dtype, with_sum):
    m = _row_reduce(parts, jnp.maximum, jnp.max)
    ps = [jnp.exp2(s - m) for s in parts]
    p = jnp.concatenate([x.astype(dtype) for x in ps], axis=1)
    return (p, _row_reduce(ps, jnp.add, jnp.sum)) if with_sum else p


def _pyramid_order(heads, n):
    items = [(h, i) for i in range(n) for h in range(heads)]
    return items[0::2] + items[1::2][::-1]


def _emit_pipelined(items, scores, softmax, output):
    n = len(items)
    s = {0: scores(items[0])}
    if n > 1:
        s[1] = scores(items[1])
    a = {0: softmax(items[0], s.pop(0))}
    for t in range(n):
        if t + 2 < n:
            s[t + 2] = scores(items[t + 2])
        if t + 1 < n:
            a[t + 1] = softmax(items[t + 1], s.pop(t + 1))
        output(items[t], a.pop(t))


def _fox_attn_kernel(q_ref, k_ref, v_ref, z_ref, c_ref, o_ref, va_ref, *, tq):
    n = q_ref.shape[0] // tq
    d = HEAD_DIM
    heads = q_ref.shape[1] // d
    mask = _causal_tile_mask(tq)
    lane = lax.broadcasted_iota(jnp.int32, (tq, LANES), 1)
    head0 = pl.program_id(1) * heads
    select = [((lane < 3 * FOX_HEADS) & (lane % FOX_HEADS == head0 + h)).astype(q_ref.dtype)
              for h in range(heads)]
    for h in range(heads):
        va_ref[h, :, :d] = v_ref[:, h * d:(h + 1) * d]
        va_ref[h, :, d:] = jnp.ones((va_ref.shape[1], d), va_ref.dtype)

    def scores(item):
        h, i = item
        lo = i * tq
        cols = slice(h * d, (h + 1) * d)
        qa = jnp.concatenate([q_ref[lo:lo + tq, cols], select[h]], axis=1)
        ka = jnp.concatenate([k_ref[:lo + tq, cols], c_ref[:lo + tq, :]], axis=1)
        s = _nt_dot(qa, ka)
        diag = jnp.where(mask, s[:, lo:], MASK_VALUE)
        return [s[:, :lo], diag] if i > 0 else [diag]

    def softmax(item, parts):
        return _exp2_softmax_rows(parts, va_ref.dtype, with_sum=False)

    def output(item, p):
        h, i = item
        hi = (i + 1) * tq
        pv = jnp.dot(p, va_ref[h, :hi, :], preferred_element_type=jnp.float32)
        cols = slice(h * d, (h + 1) * d)
        gate = _silu(z_ref[hi - tq:hi, cols].astype(jnp.float32))
        o_ref[hi - tq:hi, cols] = (pv[:, :d] / pv[:, d:] * gate).astype(o_ref.dtype)

    _emit_pipelined(_pyramid_order(heads, n), scores, softmax, output)


def _fox_attention(q, kvz, c):
    bsz, s, _ = q.shape
    h, d, hs = FOX_HEADS, HEAD_DIM, FOX_HEADS_PER_STEP
    steps = h // hs
    w = hs * d
    kern = functools.partial(_fox_attn_kernel, tq=ATTN_TQ)
    return pl.pallas_call(
        kern,
        grid=(bsz, steps),
        in_specs=[pl.BlockSpec((None, s, w), lambda b, g: (b, 0, g)),
                  pl.BlockSpec((None, s, w), lambda b, g: (b, 0, g)),
                  pl.BlockSpec((None, s, w), lambda b, g: (b, 0, steps + g)),
                  pl.BlockSpec((None, s, w), lambda b, g: (b, 0, 2 * steps + g)),
                  pl.BlockSpec((None, s, LANES), lambda b, g: (b, 0, 0))],
        out_specs=pl.BlockSpec((None, s, w), lambda b, g: (b, 0, g)),
        out_shape=jax.ShapeDtypeStruct((bsz, s, h * d), jnp.bfloat16),
        scratch_shapes=[pltpu.VMEM((hs, s, 2 * d), jnp.bfloat16)],
        compiler_params=_params("parallel", "parallel"),
        name="fox_attn",
    )(q, kvz, kvz, kvz, c)


def _bias_kernel(tab_ref, o_ref):
    h = pl.program_id(0)
    t = o_ref.shape[1]
    u = lax.broadcasted_iota(jnp.int32, (8, 2 * t), 1)
    n = jnp.where(u <= t, t - u, 3 * t - u)
    max_exact = NUM_BUCKETS // 2
    large = max_exact + (jnp.log(jnp.maximum(n, 1).astype(jnp.float32) / max_exact)
                         / math.log(MAX_DISTANCE / max_exact)
                         * (NUM_BUCKETS - max_exact)).astype(jnp.int32)
    large = jnp.minimum(large, NUM_BUCKETS - 1)
    bucket = jnp.where(n < max_exact, n, large)
    val = jnp.zeros((8, 2 * t), jnp.float32)
    for b in range(NUM_BUCKETS):
        val = jnp.where(bucket == b, tab_ref[b, h], val)
    val = (val - tab_ref[NUM_BUCKETS - 1, h]) * LOG2E
    rows = jnp.concatenate([val] * (t // 8), axis=0)
    spread = pltpu.roll(rows, 0, 1, stride=1, stride_axis=0)
    o_ref[1] = spread[:, :t]
    o_ref[0] = jnp.where(_causal_tile_mask(t), spread[:, t:], MASK_VALUE)


def _bias_tiles(rel_bias, t):
    nh = rel_bias.shape[1]
    return pl.pallas_call(
        _bias_kernel,
        grid=(nh,),
        in_specs=[pl.BlockSpec(memory_space=pltpu.SMEM)],
        out_specs=pl.BlockSpec((None, 2, t, t), lambda h: (h, 0, 0, 0)),
        out_shape=jax.ShapeDtypeStruct((nh, 2, t, t), jnp.float32),
        compiler_params=_params("parallel"),
        name="bias_tiles",
    )(rel_bias)


def _diff_attn_kernel(q_ref, k_ref, v_ref, bias_ref, lq_ref, lk_ref, g_ref, o_ref,
                      *, tq, lambda_init):
    n = q_ref.shape[0] // tq
    d = HEAD_DIM
    heads = q_ref.shape[1] // (2 * d)
    lq = lq_ref[...]
    lk = lk_ref[...]
    dots = jnp.sum(lq * lk, axis=-1, keepdims=True)
    lam = jnp.exp(dots[0:1]) - jnp.exp(dots[1:2]) + lambda_init

    def scores(item):
        h, i = item
        lo = i * tq
        maps = []
        for idx in range(2):
            cols = slice((2 * h + idx) * d, (2 * h + idx + 1) * d)
            s = _nt_dot(q_ref[lo:lo + tq, cols], k_ref[:lo + tq, cols])
            parts = [s[:, :lo - tq]] if i > 1 else []
            if i > 0:
                parts.append(s[:, lo - tq:lo] + bias_ref[h, 1])
            parts.append(s[:, lo:] + bias_ref[h, 0])
            maps.append(parts)
        return maps

    def softmax(item, maps):
        p1, l1 = _exp2_softmax_rows(maps[0], v_ref.dtype, with_sum=True)
        p2, l2 = _exp2_softmax_rows(maps[1], v_ref.dtype, with_sum=True)
        return jnp.concatenate([p1, p2], axis=0), 1.0 / l1, lam / l2

    def output(item, args):
        h, i = item
        p, r1, r2 = args
        hi = (i + 1) * tq
        cols = slice(2 * h * d, 2 * (h + 1) * d)
        pv = jnp.dot(p, v_ref[:hi, cols], preferred_element_type=jnp.float32)
        o = pv[:tq] * r1 - pv[tq:] * r2
        o = o * lax.rsqrt(jnp.mean(o * o, axis=-1, keepdims=True) + RMS_EPS) * g_ref[...]
        o_ref[hi - tq:hi, cols] = (o * (1.0 - lambda_init)).astype(o_ref.dtype)

    _emit_pipelined([(h, i) for h in range(heads) for i in range(n)], scores, softmax, output)


def _diff_attention(proj, bias, lam_q, lam_k, subln_g, lambda_init):
    bsz, s, _ = proj.shape
    t = ATTN_TQ
    h, d2, hs = DIFF_HEADS, 2 * HEAD_DIM, DIFF_HEADS_PER_STEP
    steps = h // hs
    w = hs * d2
    kern = functools.partial(_diff_attn_kernel, tq=t, lambda_init=lambda_init)
    return pl.pallas_call(
        kern,
        grid=(bsz, steps),
        in_specs=[pl.BlockSpec((None, s, w), lambda b, g: (b, 0, g)),
                  pl.BlockSpec((None, s, w), lambda b, g: (b, 0, steps + g)),
                  pl.BlockSpec((None, s, w), lambda b, g: (b, 0, 2 * steps + g)),
                  pl.BlockSpec((hs, 2, t, t), lambda b, g: (g, 0, 0, 0)),
                  pl.BlockSpec((2, HEAD_DIM), lambda b, g: (0, 0)),
                  pl.BlockSpec((2, HEAD_DIM), lambda b, g: (0, 0)),
                  pl.BlockSpec((1, d2), lambda b, g: (0, 0))],
        out_specs=pl.BlockSpec((None, s, w), lambda b, g: (b, 0, g)),
        out_shape=jax.ShapeDtypeStruct((bsz, s, h * d2), jnp.bfloat16),
        compiler_params=_params("parallel", "parallel"),
        name="diff_attn",
    )(proj, proj, proj, bias, lam_q, lam_k, subln_g.reshape(1, d2))


def _out_ln_kernel(*refs, alpha, gate_here):
    if gate_here:
        o_ref, z_ref, x_ref, w_ref, g_ref, b_ref, y_ref, *maybe_yb_ref = refs
        gated = o_ref[...].astype(jnp.float32) * _silu(z_ref[...].astype(jnp.float32))
        gated = gated.astype(w_ref.dtype)
    else:
        o_ref, x_ref, w_ref, g_ref, b_ref, y_ref, *maybe_yb_ref = refs
        gated = o_ref[...]
    d = w_ref.shape[1]
    cw = d // OUT_N_CHUNKS
    chunks = [slice(c, c + cw) for c in range(0, d, cw)]
    vs = [alpha * x_ref[:, c] + jnp.dot(gated, w_ref[:, c], preferred_element_type=jnp.float32)
          for c in chunks]
    mu = sum(jnp.sum(v, axis=-1, keepdims=True) for v in vs) / d
    vcs = [v - mu for v in vs]
    var = sum(jnp.sum(vc * vc, axis=-1, keepdims=True) for vc in vcs) / d
    rs = lax.rsqrt(var + LN_EPS)
    for c, vc in zip(chunks, vcs):
        y = vc * rs * g_ref[:, c] + b_ref[:, c]
        y_ref[:, c] = y
        for yb_ref in maybe_yb_ref:
            yb_ref[:, c] = y.astype(yb_ref.dtype)


def _out_ln(o2d, z_src, x2d, w_bf16, g, b, alpha, emit_bf16):
    m, d = x2d.shape
    br = o2d.shape[1]
    tm = OUT_TM
    kern = functools.partial(_out_ln_kernel, alpha=alpha, gate_here=z_src is not None)
    resident = dict(pipeline_mode=pl.Buffered(1))
    in_specs = [pl.BlockSpec((tm, br), lambda i: (i, 0))]
    operands = [o2d]
    if z_src is not None:
        z_block = z_src.shape[1] // br - 1
        in_specs.append(pl.BlockSpec((tm, br), lambda i: (i, z_block)))
        operands.append(z_src)
    in_specs += [pl.BlockSpec((tm, d), lambda i: (i, 0)),
                 pl.BlockSpec((br, d), lambda i: (0, 0), **resident),
                 pl.BlockSpec((1, d), lambda i: (0, 0), **resident),
                 pl.BlockSpec((1, d), lambda i: (0, 0), **resident)]
    operands += [x2d, w_bf16, g.reshape(1, d), b.reshape(1, d)]
    out_specs = [pl.BlockSpec((tm, d), lambda i: (i, 0))]
    out_shape = [jax.ShapeDtypeStruct((m, d), jnp.float32)]
    if emit_bf16:
        out_specs.append(pl.BlockSpec((tm, d), lambda i: (i, 0)))
        out_shape.append(jax.ShapeDtypeStruct((m, d), jnp.bfloat16))
    return pl.pallas_call(
        kern,
        grid=(m // tm,),
        in_specs=in_specs,
        out_specs=out_specs,
        out_shape=out_shape,
        compiler_params=_params("parallel"),
        name="out_ln",
    )(*operands)


def kernel(x, fox_w_in, fox_b_f, fox_w_out, diff_w_in, diff_lam_q, diff_lam_k,
           diff_subln_g, diff_w_out, rel_bias, ln_g, ln_b):
    bsz, s, d = x.shape
    depth = ln_g.shape[0]
    branch = fox_w_out.shape[1]
    alpha = (2 * depth) ** 0.25
    scale = HEAD_DIM ** -0.5 * LOG2E
    m = bsz * s
    assert x.dtype == jnp.float32 and fox_w_out.shape[2] == d and diff_w_out.shape[2] == d
    assert branch == FOX_HEADS * HEAD_DIM == DIFF_HEADS * 2 * HEAD_DIM == diff_w_out.shape[1]
    assert fox_w_in.shape[1:] == (d, 4 * branch + FOX_HEADS)
    assert diff_w_in.shape[1:] == (d, 4 * branch) and rel_bias.shape == (NUM_BUCKETS, DIFF_HEADS)
    assert s % FOX_Q_TS == 0 and s % ATTN_TQ == 0 and ATTN_TQ >= MAX_DISTANCE
    assert m % PROJ_TM == 0 and m % OUT_TM == 0 and branch % PROJ_TN == 0
    assert FOX_HEADS % FOX_HEADS_PER_STEP == 0 and DIFF_HEADS % DIFF_HEADS_PER_STEP == 0
    assert d % (OUT_N_CHUNKS * LANES) == 0 and 3 * FOX_HEADS <= LANES

    x2d = x.reshape(m, d)
    xb2d = None
    for i in range(depth):
        j = i // 2
        if i % 2 == 0:
            w_in_t = fox_w_in[j].T
            xb, q, c, w_out_b = _fox_q_pass(x2d.reshape(bsz, s, d), w_in_t, branch, fox_b_f[j],
                                            scale, fox_w_out[j])
            proj = _project(xb.reshape(m, d), w_in_t, branch, 3 * branch, w_transposed=True)
            o = _fox_attention(q, proj.reshape(bsz, s, 3 * branch), c)
        else:
            lambda_init = 0.8 - 0.6 * math.exp(-0.3 * i)
            proj, w_out_b = _project(xb2d, diff_w_in[j], 0, 4 * branch, w_transposed=False,
                                     q_cols=branch, scale=scale, w_out=diff_w_out[j])
            bias = _bias_tiles(rel_bias, ATTN_TQ)
            o = _diff_attention(proj.reshape(bsz, s, 4 * branch), bias, diff_lam_q[j],
                                diff_lam_k[j], diff_subln_g[j], lambda_init)
        next_is_diff = i + 1 < depth and (i + 1) % 2 == 1
        outs = _out_ln(o.reshape(m, branch), None if i % 2 == 0 else proj, x2d, w_out_b,
                       ln_g[i], ln_b[i], alpha, emit_bf16=next_is_diff)
        x2d = outs[0]
        xb2d = outs[1] if next_is_diff else None
    return x2d.reshape(bsz, s, d)
```

```python
import functools
import math

import jax
import jax.numpy as jnp
from jax import lax
from jax.experimental import pallas as pl
from jax.experimental.pallas import tpu as pltpu

LANES = 128
HEAD_DIM = 128
FOX_HEADS = 16
DIFF_HEADS = 8
NUM_BUCKETS = 32
MAX_DISTANCE = 128
LN_EPS = 1e-5
RMS_EPS = 1e-5
MASK_VALUE = -1e30
LOG2E = math.log2(math.e)

V7X_VMEM_LIMIT = 56 * 1024 * 1024

PROJ_TM = 2048
PROJ_TN = 1024
FOX_Q_TS = 512
ATTN_TQ = 256
FOX_HEADS_PER_STEP = 4
DIFF_HEADS_PER_STEP = 2
OUT_TM = 512
OUT_N_CHUNKS = 4

_NT = (((1,), (1,)), ((), ()))


def _nt_dot(a, b):
    return lax.dot_general(a, b, _NT, preferred_element_type=jnp.float32)


def _silu(z):
    return z / (1.0 + jnp.exp(-z))


def _params(*sem):
    return pltpu.CompilerParams(dimension_semantics=sem, vmem_limit_bytes=V7X_VMEM_LIMIT)


def _round_slab(wo_ref, wob_ref):
    wob_ref[...] = wo_ref[...].astype(wob_ref.dtype)


def _slab_spec(w_out, n_steps, step_index):
    assert w_out.shape[0] % (16 * n_steps) == 0
    return pl.BlockSpec((w_out.shape[0] // n_steps, w_out.shape[1]),
                        lambda *idx: (step_index(*idx), 0))


def _proj_kernel(x_ref, w_ref, *refs, n_scaled_tiles, scale, w_transposed, with_w_out):
    if with_w_out:
        wo_ref, o_ref, wob_ref, wb_ref = refs
        _round_slab(wo_ref, wob_ref)
    else:
        o_ref, wb_ref = refs

    @pl.when(pl.program_id(1) == 0)
    def _():
        wb_ref[...] = w_ref[...].astype(wb_ref.dtype)

    if w_transposed:
        acc = _nt_dot(x_ref[...], wb_ref[...])
    else:
        acc = jnp.dot(x_ref[...], wb_ref[...], preferred_element_type=jnp.float32)
    if n_scaled_tiles:
        acc = acc * jnp.where(pl.program_id(0) < n_scaled_tiles, jnp.float32(scale),
                              jnp.float32(1.0))
    o_ref[...] = acc.astype(o_ref.dtype)


def _project(xb2d, w, col0, n, *, w_transposed, q_cols=0, scale=1.0, w_out=None):
    m, k = xb2d.shape
    tm, tn = PROJ_TM, PROJ_TN
    n_i, n_j, j0 = m // tm, n // tn, col0 // tn
    kern = functools.partial(_proj_kernel, n_scaled_tiles=q_cols // tn, scale=scale,
                             w_transposed=w_transposed, with_w_out=w_out is not None)
    if w_transposed:
        w_spec = pl.BlockSpec((tn, k), lambda j, i: (j0 + j, 0))
        wb_shape = (tn, k)
    else:
        w_spec = pl.BlockSpec((k, tn), lambda j, i: (0, j0 + j))
        wb_shape = (k, tn)
    in_specs = [pl.BlockSpec((tm, k), lambda j, i: (i, 0)), w_spec]
    out_specs = [pl.BlockSpec((tm, tn), lambda j, i: (i, j))]
    out_shape = [jax.ShapeDtypeStruct((m, n), jnp.bfloat16)]
    operands = [xb2d, w]
    if w_out is not None:
        wo_spec = _slab_spec(w_out, n_j * n_i, lambda j, i: j * n_i + i)
        in_specs.append(wo_spec)
        out_specs.append(wo_spec)
        out_shape.append(jax.ShapeDtypeStruct(w_out.shape, jnp.bfloat16))
        operands.append(w_out)
    outs = pl.pallas_call(
        kern,
        grid=(n_j, n_i),
        in_specs=in_specs,
        out_specs=out_specs,
        out_shape=out_shape,
        scratch_shapes=[pltpu.VMEM(wb_shape, jnp.bfloat16)],
        compiler_params=_params("parallel", "arbitrary"),
        name="in_proj",
    )(*operands)
    return outs if w_out is not None else outs[0]


def _split3(a):
    p1 = a.astype(jnp.bfloat16)
    r1 = a - p1.astype(jnp.float32)
    p2 = r1.astype(jnp.bfloat16)
    p3 = (r1 - p2.astype(jnp.float32)).astype(jnp.bfloat16)
    return p1, p2, p3


def _fox_q_kernel(x_ref, wq_ref, wt_ref, b_ref, wo_ref, xb_ref, q_ref, c_ref, wob_ref,
                  wqb_ref, carry_ref, *, scale):
    t = pl.program_id(1)
    ts = x_ref.shape[0]
    _round_slab(wo_ref, wob_ref)

    @pl.when((pl.program_id(0) == 0) & (t == 0))
    def _():
        wqb_ref[...] = wq_ref[...].astype(wqb_ref.dtype)

    @pl.when(t == 0)
    def _():
        carry_ref[...] = jnp.zeros_like(carry_ref)

    xb = x_ref[...].astype(jnp.bfloat16)
    xb_ref[...] = xb
    q_ref[...] = (_nt_dot(xb, wqb_ref[...]) * scale).astype(q_ref.dtype)
    f = _nt_dot(wt_ref[...].astype(jnp.bfloat16), xb)
    z = f + b_ref[...]
    logf = jnp.minimum(z, 0.0) - jnp.log1p(jnp.exp(-jnp.abs(z)))
    row = lax.broadcasted_iota(jnp.int32, (ts, ts), 0)
    col = lax.broadcasted_iota(jnp.int32, (ts, ts), 1)
    tri = (row <= col).astype(jnp.bfloat16)
    h = logf.shape[0]
    sums = jnp.dot(jnp.concatenate(_split3(logf), axis=0), tri,
                   preferred_element_type=jnp.float32)
    c = (sums[:h] + sums[h:2 * h]) + sums[2 * h:] + carry_ref[...]
    pieces = [p.astype(jnp.float32) for p in _split3(c * (-LOG2E))]
    pad = jnp.zeros((LANES - 3 * h, ts), jnp.float32)
    c_ref[...] = jnp.concatenate(pieces + [pad], axis=0).T.astype(c_ref.dtype)
    carry_ref[...] = c[:, ts - 1:ts]


def _fox_q_pass(x, w_in_t, n_q, b_f, scale, w_out):
    bsz, s, d = x.shape
    h = b_f.shape[0]
    ts = FOX_Q_TS
    nt = s // ts
    gate_block = (w_in_t.shape[0] - h) // h
    once = dict(pipeline_mode=pl.Buffered(1))
    wo_spec = _slab_spec(w_out, bsz * nt, lambda b, t: b * nt + t)
    return pl.pallas_call(
        functools.partial(_fox_q_kernel, scale=scale),
        grid=(bsz, nt),
        in_specs=[pl.BlockSpec((None, ts, d), lambda b, t: (b, t, 0)),
                  pl.BlockSpec((n_q, d), lambda b, t: (0, 0), **once),
                  pl.BlockSpec((h, d), lambda b, t: (gate_block, 0), **once),
                  pl.BlockSpec((h, 1), lambda b, t: (0, 0), **once),
                  wo_spec],
        out_specs=[pl.BlockSpec((None, ts, d), lambda b, t: (b, t, 0)),
                   pl.BlockSpec((None, ts, n_q), lambda b, t: (b, t, 0)),
                   pl.BlockSpec((None, ts, LANES), lambda b, t: (b, t, 0)),
                   wo_spec],
        out_shape=[jax.ShapeDtypeStruct((bsz, s, d), jnp.bfloat16),
                   jax.ShapeDtypeStruct((bsz, s, n_q), jnp.bfloat16),
                   jax.ShapeDtypeStruct((bsz, s, LANES), jnp.bfloat16),
                   jax.ShapeDtypeStruct(w_out.shape, jnp.bfloat16)],
        scratch_shapes=[pltpu.VMEM((n_q, d), jnp.bfloat16), pltpu.VMEM((h, 1), jnp.float32)],
        compiler_params=_params("arbitrary", "arbitrary"),
        name="fox_q_pass",
    )(x, w_in_t, w_in_t, b_f.reshape(h, 1), w_out)


def _causal_tile_mask(t):
    row = lax.broadcasted_iota(jnp.int32, (t, t), 0)
    col = lax.broadcasted_iota(jnp.int32, (t, t), 1)
    return col <= row


def _row_reduce(parts, combine, lane_reduce):
    tiles = [s[:, j:j + LANES] for s in parts for j in range(0, s.shape[1], LANES)]
    return lane_reduce(functools.reduce(combine, tiles), axis=-1, keepdims=True)


def _exp2_softmax_rows(parts, dtype, with_sum):
    m = _row_reduce(parts, jnp.maximum, jnp.max)
    ps = [jnp.exp2(s - m) for s in parts]
    p = jnp.concatenate([x.astype(dtype) for x in ps], axis=1)
    return (p, _row_reduce(ps, jnp.add, jnp.sum)) if with_sum else p


def _causal_softmax_rows(parts, dtype, with_sum):
    half = parts[-1].shape[0] // 2
    top = [x[:half] for x in parts[:-1]] + [parts[-1][:half, :half]]
    bot = [x[half:] for x in parts]
    rt = _exp2_softmax_rows(top, dtype, with_sum)
    rb = _exp2_softmax_rows(bot, dtype, with_sum)
    pt, pb = (rt[0], rb[0]) if with_sum else (rt, rb)
    p = jnp.concatenate([jnp.concatenate([pt, jnp.zeros((half, half), dtype)], axis=1), pb],
                        axis=0)
    return (p, jnp.concatenate([rt[1], rb[1]], axis=0)) if with_sum else p


def _pyramid_order(heads, n):
    items = [(h, i) for i in range(n) for h in range(heads)]
    return items[0::2] + items[1::2][::-1]


def _emit_pipelined(items, scores, softmax, output):
    n = len(items)
    s = {0: scores(items[0])}
    if n > 1:
        s[1] = scores(items[1])
    a = {0: softmax(items[0], s.pop(0))}
    for t in range(n):
        if t + 2 < n:
            s[t + 2] = scores(items[t + 2])
        if t + 1 < n:
            a[t + 1] = softmax(items[t + 1], s.pop(t + 1))
        output(items[t], a.pop(t))


def _fox_attn_kernel(q_ref, k_ref, v_ref, z_ref, c_ref, o_ref, va_ref, *, tq):
    n = q_ref.shape[0] // tq
    d = HEAD_DIM
    heads = q_ref.shape[1] // d
    mask = _causal_tile_mask(tq)
    lane = lax.broadcasted_iota(jnp.int32, (tq, LANES), 1)
    head0 = pl.program_id(1) * heads
    select = [((lane < 3 * FOX_HEADS) & (lane % FOX_HEADS == head0 + h)).astype(q_ref.dtype)
              for h in range(heads)]
    for h in range(heads):
        va_ref[h, :, :d] = v_ref[:, h * d:(h + 1) * d]
        va_ref[h, :, d:] = jnp.ones((va_ref.shape[1], d), va_ref.dtype)

    def scores(item):
        h, i = item
        lo = i * tq
        cols = slice(h * d, (h + 1) * d)
        qa = jnp.concatenate([q_ref[lo:lo + tq, cols], select[h]], axis=1)
        ka = jnp.concatenate([k_ref[:lo + tq, cols], c_ref[:lo + tq, :]], axis=1)
        s = _nt_dot(qa, ka)
        diag = jnp.where(mask, s[:, lo:], MASK_VALUE)
        return [s[:, :lo], diag] if i > 0 else [diag]

    def softmax(item, parts):
        return _causal_softmax_rows(parts, va_ref.dtype, with_sum=False)

    def output(item, p):
        h, i = item
        hi = (i + 1) * tq
        pv = jnp.dot(p, va_ref[h, :hi, :], preferred_element_type=jnp.float32)
        cols = slice(h * d, (h + 1) * d)
        gate = _silu(z_ref[hi - tq:hi, cols].astype(jnp.float32))
        o_ref[hi - tq:hi, cols] = (pv[:, :d] / pv[:, d:] * gate).astype(o_ref.dtype)

    _emit_pipelined(_pyramid_order(heads, n), scores, softmax, output)


def _fox_attention(q, kvz, c):
    bsz, s, _ = q.shape
    h, d, hs = FOX_HEADS, HEAD_DIM, FOX_HEADS_PER_STEP
    steps = h // hs
    w = hs * d
    kern = functools.partial(_fox_attn_kernel, tq=ATTN_TQ)
    return pl.pallas_call(
        kern,
        grid=(bsz, steps),
        in_specs=[pl.BlockSpec((None, s, w), lambda b, g: (b, 0, g)),
                  pl.BlockSpec((None, s, w), lambda b, g: (b, 0, g)),
                  pl.BlockSpec((None, s, w), lambda b, g: (b, 0, steps + g)),
                  pl.BlockSpec((None, s, w), lambda b, g: (b, 0, 2 * steps + g)),
                  pl.BlockSpec((None, s, LANES), lambda b, g: (b, 0, 0))],
        out_specs=pl.BlockSpec((None, s, w), lambda b, g: (b, 0, g)),
        out_shape=jax.ShapeDtypeStruct((bsz, s, h * d), jnp.bfloat16),
        scratch_shapes=[pltpu.VMEM((hs, s, 2 * d), jnp.bfloat16)],
        compiler_params=_params("parallel", "parallel"),
        name="fox_attn",
    )(q, kvz, kvz, kvz, c)


def _bias_kernel(tab_ref, o_ref):
    h = pl.program_id(0)
    t = o_ref.shape[1]
    u = lax.broadcasted_iota(jnp.int32, (8, 2 * t), 1)
    n = jnp.where(u <= t, t - u, 3 * t - u)
    max_exact = NUM_BUCKETS // 2
    large = max_exact + (jnp.log(jnp.maximum(n, 1).astype(jnp.float32) / max_exact)
                         / math.log(MAX_DISTANCE / max_exact)
                         * (NUM_BUCKETS - max_exact)).astype(jnp.int32)
    large = jnp.minimum(large, NUM_BUCKETS - 1)
    bucket = jnp.where(n < max_exact, n, large)
    val = jnp.zeros((8, 2 * t), jnp.float32)
    for b in range(NUM_BUCKETS):
        val = jnp.where(bucket == b, tab_ref[b, h], val)
    val = (val - tab_ref[NUM_BUCKETS - 1, h]) * LOG2E
    rows = jnp.concatenate([val] * (t // 8), axis=0)
    spread = pltpu.roll(rows, 0, 1, stride=1, stride_axis=0)
    o_ref[1] = spread[:, :t]
    o_ref[0] = jnp.where(_causal_tile_mask(t), spread[:, t:], MASK_VALUE)


def _bias_tiles(rel_bias, t):
    nh = rel_bias.shape[1]
    return pl.pallas_call(
        _bias_kernel,
        grid=(nh,),
        in_specs=[pl.BlockSpec(memory_space=pltpu.SMEM)],
        out_specs=pl.BlockSpec((None, 2, t, t), lambda h: (h, 0, 0, 0)),
        out_shape=jax.ShapeDtypeStruct((nh, 2, t, t), jnp.float32),
        compiler_params=_params("parallel"),
        name="bias_tiles",
    )(rel_bias)


def _diff_attn_kernel(q_ref, k_ref, v_ref, bias_ref, lq_ref, lk_ref, g_ref, o_ref,
                      *, tq, lambda_init):
    n = q_ref.shape[0] // tq
    d = HEAD_DIM
    heads = q_ref.shape[1] // (2 * d)
    lq = lq_ref[...]
    lk = lk_ref[...]
    dots = jnp.sum(lq * lk, axis=-1, keepdims=True)
    lam = jnp.exp(dots[0:1]) - jnp.exp(dots[1:2]) + lambda_init

    def scores(item):
        h, i = item
        lo = i * tq
        maps = []
        for idx in range(2):
            cols = slice((2 * h + idx) * d, (2 * h + idx + 1) * d)
            s = _nt_dot(q_ref[lo:lo + tq, cols], k_ref[:lo + tq, cols])
            parts = [s[:, :lo - tq]] if i > 1 else []
            if i > 0:
                parts.append(s[:, lo - tq:lo] + bias_ref[h, 1])
            parts.append(s[:, lo:] + bias_ref[h, 0])
            maps.append(parts)
        return maps

    def softmax(item, maps):
        p1, l1 = _causal_softmax_rows(maps[0], v_ref.dtype, with_sum=True)
        p2, l2 = _causal_softmax_rows(maps[1], v_ref.dtype, with_sum=True)
        return jnp.concatenate([p1, p2], axis=0), 1.0 / l1, lam / l2

    def output(item, args):
        h, i = item
        p, r1, r2 = args
        hi = (i + 1) * tq
        cols = slice(2 * h * d, 2 * (h + 1) * d)
        pv = jnp.dot(p, v_ref[:hi, cols], preferred_element_type=jnp.float32)
        o = pv[:tq] * r1 - pv[tq:] * r2
        o = o * lax.rsqrt(jnp.mean(o * o, axis=-1, keepdims=True) + RMS_EPS) * g_ref[...]
        o_ref[hi - tq:hi, cols] = (o * (1.0 - lambda_init)).astype(o_ref.dtype)

    _emit_pipelined([(h, i) for h in range(heads) for i in range(n)], scores, softmax, output)


def _diff_attention(proj, bias, lam_q, lam_k, subln_g, lambda_init):
    bsz, s, _ = proj.shape
    t = ATTN_TQ
    h, d2, hs = DIFF_HEADS, 2 * HEAD_DIM, DIFF_HEADS_PER_STEP
    steps = h // hs
    w = hs * d2
    kern = functools.partial(_diff_attn_kernel, tq=t, lambda_init=lambda_init)
    return pl.pallas_call(
        kern,
        grid=(bsz, steps),
        in_specs=[pl.BlockSpec((None, s, w), lambda b, g: (b, 0, g)),
                  pl.BlockSpec((None, s, w), lambda b, g: (b, 0, steps + g)),
                  pl.BlockSpec((None, s, w), lambda b, g: (b, 0, 2 * steps + g)),
                  pl.BlockSpec((hs, 2, t, t), lambda b, g: (g, 0, 0, 0)),
                  pl.BlockSpec((2, HEAD_DIM), lambda b, g: (0, 0)),
                  pl.BlockSpec((2, HEAD_DIM), lambda b, g: (0, 0)),
                  pl.BlockSpec((1, d2), lambda b, g: (0, 0))],
        out_specs=pl.BlockSpec((None, s, w), lambda b, g: (b, 0, g)),
        out_shape=jax.ShapeDtypeStruct((bsz, s, h * d2), jnp.bfloat16),
        compiler_params=_params("parallel", "parallel"),
        name="diff_attn",
    )(proj, proj, proj, bias, lam_q, lam_k, subln_g.reshape(1, d2))


def _out_ln_kernel(*refs, alpha, gate_here):
    if gate_here:
        o_ref, z_ref, x_ref, w_ref, g_ref, b_ref, y_ref, *maybe_yb_ref = refs
        gated = o_ref[...].astype(jnp.float32) * _silu(z_ref[...].astype(jnp.float32))
        gated = gated.astype(w_ref.dtype)
    else:
        o_ref, x_ref, w_ref, g_ref, b_ref, y_ref, *maybe_yb_ref = refs
        gated = o_ref[...]
    d = w_ref.shape[1]
    cw = d // OUT_N_CHUNKS
    chunks = [slice(c, c + cw) for c in range(0, d, cw)]
    vs = [alpha * x_ref[:, c] + jnp.dot(gated, w_ref[:, c], preferred_element_type=jnp.float32)
          for c in chunks]
    mu = sum(jnp.sum(v, axis=-1, keepdims=True) for v in vs) / d
    vcs = [v - mu for v in vs]
    var = sum(jnp.sum(vc * vc, axis=-1, keepdims=True) for vc in vcs) / d
    rs = lax.rsqrt(var + LN_EPS)
    for c, vc in zip(chunks, vcs):
        y = vc * rs * g_ref[:, c] + b_ref[:, c]
        y_ref[:, c] = y
        for yb_ref in maybe_yb_ref:
            yb_ref[:, c] = y.astype(yb_ref.dtype)


def _out_ln(o2d, z_src, x2d, w_bf16, g, b, alpha, emit_bf16):
    m, d = x2d.shape
    br = o2d.shape[1]
    tm = OUT_TM
    kern = functools.partial(_out_ln_kernel, alpha=alpha, gate_here=z_src is not None)
    resident = dict(pipeline_mode=pl.Buffered(1))
    in_specs = [pl.BlockSpec((tm, br), lambda i: (i, 0))]
    operands = [o2d]
    if z_src is not None:
        z_block = z_src.shape[1] // br - 1
        in_specs.append(pl.BlockSpec((tm, br), lambda i: (i, z_block)))
        operands.append(z_src)
    in_specs += [pl.BlockSpec((tm, d), lambda i: (i, 0)),
                 pl.BlockSpec((br, d), lambda i: (0, 0), **resident),
                 pl.BlockSpec((1, d), lambda i: (0, 0), **resident),
                 pl.BlockSpec((1, d), lambda i: (0, 0), **resident)]
    operands += [x2d, w_bf16, g.reshape(1, d), b.reshape(1, d)]
    out_specs = [pl.BlockSpec((tm, d), lambda i: (i, 0))]
    out_shape = [jax.ShapeDtypeStruct((m, d), jnp.float32)]
    if emit_bf16:
        out_specs.append(pl.BlockSpec((tm, d), lambda i: (i, 0)))
        out_shape.append(jax.ShapeDtypeStruct((m, d), jnp.bfloat16))
    return pl.pallas_call(
        kern,
        grid=(m // tm,),
        in_specs=in_specs,
        out_specs=out_specs,
        out_shape=out_shape,
        compiler_params=_params("parallel"),
        name="out_ln",
    )(*operands)


def kernel(x, fox_w_in, fox_b_f, fox_w_out, diff_w_in, diff_lam_q, diff_lam_k,
           diff_subln_g, diff_w_out, rel_bias, ln_g, ln_b):
    bsz, s, d = x.shape
    depth = ln_g.shape[0]
    branch = fox_w_out.shape[1]
    alpha = (2 * depth) ** 0.25
    scale = HEAD_DIM ** -0.5 * LOG2E
    m = bsz * s
    assert x.dtype == jnp.float32 and fox_w_out.shape[2] == d and diff_w_out.shape[2] == d
    assert branch == FOX_HEADS * HEAD_DIM == DIFF_HEADS * 2 * HEAD_DIM == diff_w_out.shape[1]
    assert fox_w_in.shape[1:] == (d, 4 * branch + FOX_HEADS)
    assert diff_w_in.shape[1:] == (d, 4 * branch) and rel_bias.shape == (NUM_BUCKETS, DIFF_HEADS)
    assert s % FOX_Q_TS == 0 and s % ATTN_TQ == 0 and ATTN_TQ >= MAX_DISTANCE
    assert m % PROJ_TM == 0 and m % OUT_TM == 0 and branch % PROJ_TN == 0
    assert FOX_HEADS % FOX_HEADS_PER_STEP == 0 and DIFF_HEADS % DIFF_HEADS_PER_STEP == 0
    assert d % (OUT_N_CHUNKS * LANES) == 0 and 3 * FOX_HEADS <= LANES

    x2d = x.reshape(m, d)
    xb2d = None
    for i in range(depth):
        j = i // 2
        if i % 2 == 0:
            w_in_t = fox_w_in[j].T
            xb, q, c, w_out_b = _fox_q_pass(x2d.reshape(bsz, s, d), w_in_t, branch, fox_b_f[j],
                                            scale, fox_w_out[j])
            proj = _project(xb.reshape(m, d), w_in_t, branch, 3 * branch, w_transposed=True)
            o = _fox_attention(q, proj.reshape(bsz, s, 3 * branch), c)
        else:
            lambda_init = 0.8 - 0.6 * math.exp(-0.3 * i)
            proj, w_out_b = _project(xb2d, diff_w_in[j], 0, 4 * branch, w_transposed=False,
                                     q_cols=branch, scale=scale, w_out=diff_w_out[j])
            bias = _bias_tiles(rel_bias, ATTN_TQ)
            o = _diff_attention(proj.reshape(bsz, s, 4 * branch), bias, diff_lam_q[j],
                                diff_lam_k[j], diff_subln_g[j], lambda_init)
        next_is_diff = i + 1 < depth and (i + 1) % 2 == 1
        outs = _out_ln(o.reshape(m, branch), None if i % 2 == 0 else proj, x2d, w_out_b,
                       ln_g[i], ln_b[i], alpha, emit_bf16=next_is_diff)
        x2d = outs[0]
        xb2d = outs[1] if next_is_diff else None
    return x2d.reshape(bsz, s, d)
```

```python
import functools
import math

import jax
import jax.numpy as jnp
from jax import lax
from jax.experimental import pallas as pl
from jax.experimental.pallas import tpu as pltpu

LANES = 128
HEAD_DIM = 128
FOX_HEADS = 16
DIFF_HEADS = 8
NUM_BUCKETS = 32
MAX_DISTANCE = 128
LN_EPS = 1e-5
RMS_EPS = 1e-5
MASK_VALUE = -1e30
LOG2E = math.log2(math.e)

V7X_VMEM_LIMIT = 56 * 1024 * 1024

PROJ_TM = 2048
PROJ_TN = 1024
FOX_Q_TS = 512
ATTN_TQ = 256
FOX_HEADS_PER_STEP = 4
DIFF_HEADS_PER_STEP = 2
OUT_TM = 512
OUT_N_CHUNKS = 4

_NT = (((1,), (1,)), ((), ()))


def _nt_dot(a, b):
    return lax.dot_general(a, b, _NT, preferred_element_type=jnp.float32)


def _silu(z):
    return z / (1.0 + jnp.exp(-z))


def _params(*sem):
    return pltpu.CompilerParams(dimension_semantics=sem, vmem_limit_bytes=V7X_VMEM_LIMIT)


def _round_slab(wo_ref, wob_ref):
    wob_ref[...] = wo_ref[...].astype(wob_ref.dtype)


def _slab_spec(w_out, n_steps, step_index):
    assert w_out.shape[0] % (16 * n_steps) == 0
    return pl.BlockSpec((w_out.shape[0] // n_steps, w_out.shape[1]),
                        lambda *idx: (step_index(*idx), 0))


def _proj_kernel(x_ref, w_ref, *refs, n_scaled_tiles, scale, w_transposed, with_w_out):
    if with_w_out:
        wo_ref, o_ref, wob_ref, wb_ref = refs
        _round_slab(wo_ref, wob_ref)
    else:
        o_ref, wb_ref = refs

    @pl.when(pl.program_id(1) == 0)
    def _():
        wb_ref[...] = w_ref[...].astype(wb_ref.dtype)

    if w_transposed:
        acc = _nt_dot(x_ref[...], wb_ref[...])
    else:
        acc = jnp.dot(x_ref[...], wb_ref[...], preferred_element_type=jnp.float32)
    if n_scaled_tiles:
        acc = acc * jnp.where(pl.program_id(0) < n_scaled_tiles, jnp.float32(scale),
                              jnp.float32(1.0))
    o_ref[...] = acc.astype(o_ref.dtype)


def _project(xb2d, w, col0, n, *, w_transposed, q_cols=0, scale=1.0, w_out=None):
    m, k = xb2d.shape
    tm, tn = PROJ_TM, PROJ_TN
    n_i, n_j, j0 = m // tm, n // tn, col0 // tn
    kern = functools.partial(_proj_kernel, n_scaled_tiles=q_cols // tn, scale=scale,
                             w_transposed=w_transposed, with_w_out=w_out is not None)
    if w_transposed:
        w_spec = pl.BlockSpec((tn, k), lambda j, i: (j0 + j, 0))
        wb_shape = (tn, k)
    else:
        w_spec = pl.BlockSpec((k, tn), lambda j, i: (0, j0 + j))
        wb_shape = (k, tn)
    in_specs = [pl.BlockSpec((tm, k), lambda j, i: (i, 0)), w_spec]
    out_specs = [pl.BlockSpec((tm, tn), lambda j, i: (i, j))]
    out_shape = [jax.ShapeDtypeStruct((m, n), jnp.bfloat16)]
    operands = [xb2d, w]
    if w_out is not None:
        wo_spec = _slab_spec(w_out, n_j * n_i, lambda j, i: j * n_i + i)
        in_specs.append(wo_spec)
        out_specs.append(wo_spec)
        out_shape.append(jax.ShapeDtypeStruct(w_out.shape, jnp.bfloat16))
        operands.append(w_out)
    outs = pl.pallas_call(
        kern,
        grid=(n_j, n_i),
        in_specs=in_specs,
        out_specs=out_specs,
        out_shape=out_shape,
        scratch_shapes=[pltpu.VMEM(wb_shape, jnp.bfloat16)],
        compiler_params=_params("parallel", "arbitrary"),
        name="in_proj",
    )(*operands)
    return outs if w_out is not None else outs[0]


def _split3(a):
    p1 = a.astype(jnp.bfloat16)
    r1 = a - p1.astype(jnp.float32)
    p2 = r1.astype(jnp.bfloat16)
    p3 = (r1 - p2.astype(jnp.float32)).astype(jnp.bfloat16)
    return p1, p2, p3


def _fox_q_kernel(x_ref, wq_ref, wt_ref, b_ref, wo_ref, xb_ref, q_ref, c_ref, wob_ref,
                  wqb_ref, carry_ref, *, scale):
    t = pl.program_id(1)
    ts = x_ref.shape[0]
    _round_slab(wo_ref, wob_ref)

    @pl.when((pl.program_id(0) == 0) & (t == 0))
    def _():
        wqb_ref[...] = wq_ref[...].astype(wqb_ref.dtype)

    @pl.when(t == 0)
    def _():
        carry_ref[...] = jnp.zeros_like(carry_ref)

    xb = x_ref[...].astype(jnp.bfloat16)
    xb_ref[...] = xb
    q_ref[...] = (_nt_dot(xb, wqb_ref[...]) * scale).astype(q_ref.dtype)
    f = _nt_dot(wt_ref[...].astype(jnp.bfloat16), xb)
    z = f + b_ref[...]
    logf = jnp.minimum(z, 0.0) - jnp.log1p(jnp.exp(-jnp.abs(z)))
    row = lax.broadcasted_iota(jnp.int32, (ts, ts), 0)
    col = lax.broadcasted_iota(jnp.int32, (ts, ts), 1)
    tri = (row <= col).astype(jnp.bfloat16)
    h = logf.shape[0]
    sums = jnp.dot(jnp.concatenate(_split3(logf), axis=0), tri,
                   preferred_element_type=jnp.float32)
    c = (sums[:h] + sums[h:2 * h]) + sums[2 * h:] + carry_ref[...]
    pieces = [p.astype(jnp.float32) for p in _split3(c * (-LOG2E))]
    pad = jnp.zeros((LANES - 3 * h, ts), jnp.float32)
    c_ref[...] = jnp.concatenate(pieces + [pad], axis=0).T.astype(c_ref.dtype)
    carry_ref[...] = c[:, ts - 1:ts]


def _fox_q_pass(x, w_in_t, n_q, b_f, scale, w_out):
    bsz, s, d = x.shape
    h = b_f.shape[0]
    ts = FOX_Q_TS
    nt = s // ts
    gate_block = (w_in_t.shape[0] - h) // h
    once = dict(pipeline_mode=pl.Buffered(1))
    wo_spec = _slab_spec(w_out, bsz * nt, lambda b, t: b * nt + t)
    return pl.pallas_call(
        functools.partial(_fox_q_kernel, scale=scale),
        grid=(bsz, nt),
        in_specs=[pl.BlockSpec((None, ts, d), lambda b, t: (b, t, 0)),
                  pl.BlockSpec((n_q, d), lambda b, t: (0, 0), **once),
                  pl.BlockSpec((h, d), lambda b, t: (gate_block, 0), **once),
                  pl.BlockSpec((h, 1), lambda b, t: (0, 0), **once),
                  wo_spec],
        out_specs=[pl.BlockSpec((None, ts, d), lambda b, t: (b, t, 0)),
                   pl.BlockSpec((None, ts, n_q), lambda b, t: (b, t, 0)),
                   pl.BlockSpec((None, ts, LANES), lambda b, t: (b, t, 0)),
                   wo_spec],
        out_shape=[jax.ShapeDtypeStruct((bsz, s, d), jnp.bfloat16),
                   jax.ShapeDtypeStruct((bsz, s, n_q), jnp.bfloat16),
                   jax.ShapeDtypeStruct((bsz, s, LANES), jnp.bfloat16),
                   jax.ShapeDtypeStruct(w_out.shape, jnp.bfloat16)],
        scratch_shapes=[pltpu.VMEM((n_q, d), jnp.bfloat16), pltpu.VMEM((h, 1), jnp.float32)],
        compiler_params=_params("arbitrary", "arbitrary"),
        name="fox_q_pass",
    )(x, w_in_t, w_in_t, b_f.reshape(h, 1), w_out)


def _causal_tile_mask(t):
    row = lax.broadcasted_iota(jnp.int32, (t, t), 0)
    col = lax.broadcasted_iota(jnp.int32, (t, t), 1)
    return col <= row


def _row_reduce(parts, combine, lane_reduce):
    tiles = [s[:, j:j + LANES] for s in parts for j in range(0, s.shape[1], LANES)]
    return lane_reduce(functools.reduce(combine, tiles), axis=-1, keepdims=True)


def _exp2_softmax_rows(parts, dtype, with_sum):
    m = _row_reduce(parts, jnp.maximum, jnp.max)
    ps = [jnp.exp2(s - m) for s in parts]
    p = jnp.concatenate([x.astype(dtype) for x in ps], axis=1)
    return (p, _row_reduce(ps, jnp.add, jnp.sum)) if with_sum else p


def _causal_softmax_rows(parts, dtype, with_sum):
    half = parts[-1].shape[0] // 2
    top = [x[:half] for x in parts[:-1]] + [parts[-1][:half, :half]]
    bot = [x[half:] for x in parts]
    rt = _exp2_softmax_rows(top, dtype, with_sum)
    rb = _exp2_softmax_rows(bot, dtype, with_sum)
    pt, pb = (rt[0], rb[0]) if with_sum else (rt, rb)
    p = jnp.concatenate([jnp.concatenate([pt, jnp.zeros((half, half), dtype)], axis=1), pb],
                        axis=0)
    return (p, jnp.concatenate([rt[1], rb[1]], axis=0)) if with_sum else p


def _pyramid_order(heads, n):
    items = [(h, i) for i in range(n) for h in range(heads)]
    return items[0::2] + items[1::2][::-1]


def _emit_pipelined(items, scores, softmax, output):
    n = len(items)
    s = {0: scores(items[0])}
    if n > 1:
        s[1] = scores(items[1])
    a = {0: softmax(items[0], s.pop(0))}
    for t in range(n):
        if t + 2 < n:
            s[t + 2] = scores(items[t + 2])
        if t + 1 < n:
            a[t + 1] = softmax(items[t + 1], s.pop(t + 1))
        output(items[t], a.pop(t))


def _fox_attn_kernel(q_ref, k_ref, v_ref, z_ref, c_ref, o_ref, va_ref, *, tq):
    n = q_ref.shape[0] // tq
    d = HEAD_DIM
    heads = q_ref.shape[1] // d
    mask = _causal_tile_mask(tq)
    lane = lax.broadcasted_iota(jnp.int32, (tq, LANES), 1)
    head0 = pl.program_id(1) * heads
    select = [((lane < 3 * FOX_HEADS) & (lane % FOX_HEADS == head0 + h)).astype(q_ref.dtype)
              for h in range(heads)]
    for h in range(heads):
        va_ref[h, :, :d] = v_ref[:, h * d:(h + 1) * d]
        va_ref[h, :, d:] = jnp.ones((va_ref.shape[1], d), va_ref.dtype)

    def scores(item):
        h, i = item
        lo = i * tq
        cols = slice(h * d, (h + 1) * d)
        qa = jnp.concatenate([q_ref[lo:lo + tq, cols], select[h]], axis=1)
        ka = jnp.concatenate([k_ref[:lo + tq, cols], c_ref[:lo + tq, :]], axis=1)
        s = _nt_dot(qa, ka)
        diag = jnp.where(mask, s[:, lo:], MASK_VALUE)
        return [s[:, :lo], diag] if i > 0 else [diag]

    def softmax(item, parts):
        return _causal_softmax_rows(parts, va_ref.dtype, with_sum=False)

    def output(item, p):
        h, i = item
        hi = (i + 1) * tq
        pv = jnp.dot(p, va_ref[h, :hi, :], preferred_element_type=jnp.float32)
        cols = slice(h * d, (h + 1) * d)
        gate = _silu(z_ref[hi - tq:hi, cols].astype(jnp.float32))
        o_ref[hi - tq:hi, cols] = (pv[:, :d] / pv[:, d:] * gate).astype(o_ref.dtype)

    _emit_pipelined(_pyramid_order(heads, n), scores, softmax, output)


def _fox_attention(q, kvz, c):
    bsz, s, _ = q.shape
    h, d, hs = FOX_HEADS, HEAD_DIM, FOX_HEADS_PER_STEP
    steps = h // hs
    w = hs * d
    kern = functools.partial(_fox_attn_kernel, tq=ATTN_TQ)
    return pl.pallas_call(
        kern,
        grid=(bsz, steps),
        in_specs=[pl.BlockSpec((None, s, w), lambda b, g: (b, 0, g)),
                  pl.BlockSpec((None, s, w), lambda b, g: (b, 0, g)),
                  pl.BlockSpec((None, s, w), lambda b, g: (b, 0, steps + g)),
                  pl.BlockSpec((None, s, w), lambda b, g: (b, 0, 2 * steps + g)),
                  pl.BlockSpec((None, s, LANES), lambda b, g: (b, 0, 0))],
        out_specs=pl.BlockSpec((None, s, w), lambda b, g: (b, 0, g)),
        out_shape=jax.ShapeDtypeStruct((bsz, s, h * d), jnp.bfloat16),
        scratch_shapes=[pltpu.VMEM((hs, s, 2 * d), jnp.bfloat16)],
        compiler_params=_params("parallel", "parallel"),
        name="fox_attn",
    )(q, kvz, kvz, kvz, c)


def _bias_kernel(tab_ref, o_ref):
    h = pl.program_id(0)
    t = o_ref.shape[1]
    u = lax.broadcasted_iota(jnp.int32, (8, 2 * t), 1)
    n = jnp.where(u <= t, t - u, 3 * t - u)
    max_exact = NUM_BUCKETS // 2
    large = max_exact + (jnp.log(jnp.maximum(n, 1).astype(jnp.float32) / max_exact)
                         / math.log(MAX_DISTANCE / max_exact)
                         * (NUM_BUCKETS - max_exact)).astype(jnp.int32)
    large = jnp.minimum(large, NUM_BUCKETS - 1)
    bucket = jnp.where(n < max_exact, n, large)
    val = jnp.zeros((8, 2 * t), jnp.float32)
    for b in range(NUM_BUCKETS):
        val = jnp.where(bucket == b, tab_ref[b, h], val)
    val = (val - tab_ref[NUM_BUCKETS - 1, h]) * LOG2E
    rows = jnp.concatenate([val] * (t // 8), axis=0)
    spread = pltpu.roll(rows, 0, 1, stride=1, stride_axis=0)
    o_ref[1] = spread[:, :t]
    o_ref[0] = jnp.where(_causal_tile_mask(t), spread[:, t:], MASK_VALUE)


def _bias_tiles(rel_bias, t):
    nh = rel_bias.shape[1]
    return pl.pallas_call(
        _bias_kernel,
        grid=(nh,),
        in_specs=[pl.BlockSpec(memory_space=pltpu.SMEM)],
        out_specs=pl.BlockSpec((None, 2, t, t), lambda h: (h, 0, 0, 0)),
        out_shape=jax.ShapeDtypeStruct((nh, 2, t, t), jnp.float32),
        compiler_params=_params("parallel"),
        name="bias_tiles",
    )(rel_bias)


def _diff_attn_kernel(q_ref, k_ref, v_ref, bias_ref, lq_ref, lk_ref, g_ref, o_ref,
                      *, tq, lambda_init):
    n = q_ref.shape[0] // tq
    d = HEAD_DIM
    heads = q_ref.shape[1] // (2 * d)
    lq = lq_ref[...]
    lk = lk_ref[...]
    dots = jnp.sum(lq * lk, axis=-1, keepdims=True)
    lam = jnp.exp(dots[0:1]) - jnp.exp(dots[1:2]) + lambda_init

    def scores(item):
        h, i = item
        lo = i * tq
        half = tq // 2
        maps = []
        for idx in range(2):
            cols = slice((2 * h + idx) * d, (2 * h + idx + 1) * d)
            s = _nt_dot(q_ref[lo:lo + tq, cols], k_ref[:lo + tq, cols])
            parts = [s[:, :lo - tq]] if i > 1 else []
            if i > 0:
                near = s[:, lo - tq:lo]
                corner = near[:half, half:] + bias_ref[h, 1, :half, half:]
                top = jnp.concatenate([near[:half, :half], corner], axis=1)
                parts.append(jnp.concatenate([top, near[half:]], axis=0))
            parts.append(s[:, lo:] + bias_ref[h, 0])
            maps.append(parts)
        return maps

    def softmax(item, maps):
        p1, l1 = _causal_softmax_rows(maps[0], v_ref.dtype, with_sum=True)
        p2, l2 = _causal_softmax_rows(maps[1], v_ref.dtype, with_sum=True)
        return jnp.concatenate([p1, p2], axis=0), 1.0 / l1, lam / l2

    def output(item, args):
        h, i = item
        p, r1, r2 = args
        hi = (i + 1) * tq
        cols = slice(2 * h * d, 2 * (h + 1) * d)
        pv = jnp.dot(p, v_ref[:hi, cols], preferred_element_type=jnp.float32)
        o = pv[:tq] * r1 - pv[tq:] * r2
        o = o * lax.rsqrt(jnp.mean(o * o, axis=-1, keepdims=True) + RMS_EPS) * g_ref[...]
        o_ref[hi - tq:hi, cols] = (o * (1.0 - lambda_init)).astype(o_ref.dtype)

    _emit_pipelined([(h, i) for h in range(heads) for i in range(n)], scores, softmax, output)


def _diff_attention(proj, bias, lam_q, lam_k, subln_g, lambda_init):
    bsz, s, _ = proj.shape
    t = ATTN_TQ
    h, d2, hs = DIFF_HEADS, 2 * HEAD_DIM, DIFF_HEADS_PER_STEP
    steps = h // hs
    w = hs * d2
    kern = functools.partial(_diff_attn_kernel, tq=t, lambda_init=lambda_init)
    return pl.pallas_call(
        kern,
        grid=(bsz, steps),
        in_specs=[pl.BlockSpec((None, s, w), lambda b, g: (b, 0, g)),
                  pl.BlockSpec((None, s, w), lambda b, g: (b, 0, steps + g)),
                  pl.BlockSpec((None, s, w), lambda b, g: (b, 0, 2 * steps + g)),
                  pl.BlockSpec((hs, 2, t, t), lambda b, g: (g, 0, 0, 0)),
                  pl.BlockSpec((2, HEAD_DIM), lambda b, g: (0, 0)),
                  pl.BlockSpec((2, HEAD_DIM), lambda b, g: (0, 0)),
                  pl.BlockSpec((1, d2), lambda b, g: (0, 0))],
        out_specs=pl.BlockSpec((None, s, w), lambda b, g: (b, 0, g)),
        out_shape=jax.ShapeDtypeStruct((bsz, s, h * d2), jnp.bfloat16),
        compiler_params=_params("parallel", "parallel"),
        name="diff_attn",
    )(proj, proj, proj, bias, lam_q, lam_k, subln_g.reshape(1, d2))


def _out_ln_kernel(*refs, alpha, gate_here):
    if gate_here:
        o_ref, z_ref, x_ref, w_ref, g_ref, b_ref, y_ref, *maybe_yb_ref = refs
        gated = o_ref[...].astype(jnp.float32) * _silu(z_ref[...].astype(jnp.float32))
        gated = gated.astype(w_ref.dtype)
    else:
        o_ref, x_ref, w_ref, g_ref, b_ref, y_ref, *maybe_yb_ref = refs
        gated = o_ref[...]
    d = w_ref.shape[1]
    cw = d // OUT_N_CHUNKS
    chunks = [slice(c, c + cw) for c in range(0, d, cw)]
    vs = [alpha * x_ref[:, c] + jnp.dot(gated, w_ref[:, c], preferred_element_type=jnp.float32)
          for c in chunks]
    mu = sum(jnp.sum(v, axis=-1, keepdims=True) for v in vs) / d
    vcs = [v - mu for v in vs]
    var = sum(jnp.sum(vc * vc, axis=-1, keepdims=True) for vc in vcs) / d
    rs = lax.rsqrt(var + LN_EPS)
    for c, vc in zip(chunks, vcs):
        y = vc * rs * g_ref[:, c] + b_ref[:, c]
        y_ref[:, c] = y
        for yb_ref in maybe_yb_ref:
            yb_ref[:, c] = y.astype(yb_ref.dtype)


def _out_ln(o2d, z_src, x2d, w_bf16, g, b, alpha, emit_bf16):
    m, d = x2d.shape
    br = o2d.shape[1]
    tm = OUT_TM
    kern = functools.partial(_out_ln_kernel, alpha=alpha, gate_here=z_src is not None)
    resident = dict(pipeline_mode=pl.Buffered(1))
    in_specs = [pl.BlockSpec((tm, br), lambda i: (i, 0))]
    operands = [o2d]
    if z_src is not None:
        z_block = z_src.shape[1] // br - 1
        in_specs.append(pl.BlockSpec((tm, br), lambda i: (i, z_block)))
        operands.append(z_src)
    in_specs += [pl.BlockSpec((tm, d), lambda i: (i, 0)),
                 pl.BlockSpec((br, d), lambda i: (0, 0), **resident),
                 pl.BlockSpec((1, d), lambda i: (0, 0), **resident),
                 pl.BlockSpec((1, d), lambda i: (0, 0), **resident)]
    operands += [x2d, w_bf16, g.reshape(1, d), b.reshape(1, d)]
    out_specs = [pl.BlockSpec((tm, d), lambda i: (i, 0))]
    out_shape = [jax.ShapeDtypeStruct((m, d), jnp.float32)]
    if emit_bf16:
        out_specs.append(pl.BlockSpec((tm, d), lambda i: (i, 0)))
        out_shape.append(jax.ShapeDtypeStruct((m, d), jnp.bfloat16))
    return pl.pallas_call(
        kern,
        grid=(m // tm,),
        in_specs=in_specs,
        out_specs=out_specs,
        out_shape=out_shape,
        compiler_params=_params("parallel"),
        name="out_ln",
    )(*operands)


def kernel(x, fox_w_in, fox_b_f, fox_w_out, diff_w_in, diff_lam_q, diff_lam_k,
           diff_subln_g, diff_w_out, rel_bias, ln_g, ln_b):
    bsz, s, d = x.shape
    depth = ln_g.shape[0]
    branch = fox_w_out.shape[1]
    alpha = (2 * depth) ** 0.25
    scale = HEAD_DIM ** -0.5 * LOG2E
    m = bsz * s
    assert x.dtype == jnp.float32 and fox_w_out.shape[2] == d and diff_w_out.shape[2] == d
    assert branch == FOX_HEADS * HEAD_DIM == DIFF_HEADS * 2 * HEAD_DIM == diff_w_out.shape[1]
    assert fox_w_in.shape[1:] == (d, 4 * branch + FOX_HEADS)
    assert diff_w_in.shape[1:] == (d, 4 * branch) and rel_bias.shape == (NUM_BUCKETS, DIFF_HEADS)
    assert s % FOX_Q_TS == 0 and s % ATTN_TQ == 0 and ATTN_TQ >= 2 * MAX_DISTANCE
    assert m % PROJ_TM == 0 and m % OUT_TM == 0 and branch % PROJ_TN == 0
    assert FOX_HEADS % FOX_HEADS_PER_STEP == 0 and DIFF_HEADS % DIFF_HEADS_PER_STEP == 0
    assert d % (OUT_N_CHUNKS * LANES) == 0 and 3 * FOX_HEADS <= LANES

    x2d = x.reshape(m, d)
    xb2d = None
    for i in range(depth):
        j = i // 2
        if i % 2 == 0:
            w_in_t = fox_w_in[j].T
            xb, q, c, w_out_b = _fox_q_pass(x2d.reshape(bsz, s, d), w_in_t, branch, fox_b_f[j],
                                            scale, fox_w_out[j])
            proj = _project(xb.reshape(m, d), w_in_t, branch, 3 * branch, w_transposed=True)
            o = _fox_attention(q, proj.reshape(bsz, s, 3 * branch), c)
        else:
            lambda_init = 0.8 - 0.6 * math.exp(-0.3 * i)
            proj, w_out_b = _project(xb2d, diff_w_in[j], 0, 4 * branch, w_transposed=False,
                                     q_cols=branch, scale=scale, w_out=diff_w_out[j])
            bias = _bias_tiles(rel_bias, ATTN_TQ)
            o = _diff_attention(proj.reshape(bsz, s, 4 * branch), bias, diff_lam_q[j],
                                diff_lam_k[j], diff_subln_g[j], lambda_init)
        next_is_diff = i + 1 < depth and (i + 1) % 2 == 1
        outs = _out_ln(o.reshape(m, branch), None if i % 2 == 0 else proj, x2d, w_out_b,
                       ln_g[i], ln_b[i], alpha, emit_bf16=next_is_diff)
        x2d = outs[0]
        xb2d = outs[1] if next_is_diff else None
    return x2d.reshape(bsz, s, d)
```

```python
import functools
import math

import jax
import jax.numpy as jnp
from jax import lax
from jax.experimental import pallas as pl
from jax.experimental.pallas import tpu as pltpu

LANES = 128
HEAD_DIM = 128
FOX_HEADS = 16
DIFF_HEADS = 8
NUM_BUCKETS = 32
MAX_DISTANCE = 128
LN_EPS = 1e-5
RMS_EPS = 1e-5
MASK_VALUE = -1e30
LOG2E = math.log2(math.e)

V7X_VMEM_LIMIT = 56 * 1024 * 1024

PROJ_TM = 2048
PROJ_TN = 1024
FOX_Q_TS = 512
ATTN_TQ = 256
FOX_HEADS_PER_STEP = 4
DIFF_HEADS_PER_STEP = 2
OUT_TM = 512
OUT_N_CHUNKS = 4

_NT = (((1,), (1,)), ((), ()))


def _nt_dot(a, b):
    return lax.dot_general(a, b, _NT, preferred_element_type=jnp.float32)


def _silu(z):
    return z / (1.0 + jnp.exp(-z))


def _params(*sem):
    return pltpu.CompilerParams(dimension_semantics=sem, vmem_limit_bytes=V7X_VMEM_LIMIT)


def _round_slab(wo_ref, wob_ref):
    wob_ref[...] = wo_ref[...].astype(wob_ref.dtype)


def _slab_spec(w_out, n_steps, step_index):
    assert w_out.shape[0] % (16 * n_steps) == 0
    return pl.BlockSpec((w_out.shape[0] // n_steps, w_out.shape[1]),
                        lambda *idx: (step_index(*idx), 0))


def _proj_kernel(x_ref, w_ref, *refs, n_scaled_tiles, scale, w_transposed, with_w_out):
    if with_w_out:
        wo_ref, o_ref, wob_ref, wb_ref = refs
        _round_slab(wo_ref, wob_ref)
    else:
        o_ref, wb_ref = refs

    @pl.when(pl.program_id(1) == 0)
    def _():
        wb_ref[...] = w_ref[...].astype(wb_ref.dtype)

    if w_transposed:
        acc = _nt_dot(x_ref[...], wb_ref[...])
    else:
        acc = jnp.dot(x_ref[...], wb_ref[...], preferred_element_type=jnp.float32)
    if n_scaled_tiles:
        acc = acc * jnp.where(pl.program_id(0) < n_scaled_tiles, jnp.float32(scale),
                              jnp.float32(1.0))
    o_ref[...] = acc.astype(o_ref.dtype)


def _project(xb2d, w, col0, n, *, w_transposed, q_cols=0, scale=1.0, w_out=None):
    m, k = xb2d.shape
    tm, tn = PROJ_TM, PROJ_TN
    n_i, n_j, j0 = m // tm, n // tn, col0 // tn
    kern = functools.partial(_proj_kernel, n_scaled_tiles=q_cols // tn, scale=scale,
                             w_transposed=w_transposed, with_w_out=w_out is not None)
    if w_transposed:
        w_spec = pl.BlockSpec((tn, k), lambda j, i: (j0 + j, 0))
        wb_shape = (tn, k)
    else:
        w_spec = pl.BlockSpec((k, tn), lambda j, i: (0, j0 + j))
        wb_shape = (k, tn)
    in_specs = [pl.BlockSpec((tm, k), lambda j, i: (i, 0)), w_spec]
    out_specs = [pl.BlockSpec((tm, tn), lambda j, i: (i, j))]
    out_shape = [jax.ShapeDtypeStruct((m, n), jnp.bfloat16)]
    operands = [xb2d, w]
    if w_out is not None:
        wo_spec = _slab_spec(w_out, n_j * n_i, lambda j, i: j * n_i + i)
        in_specs.append(wo_spec)
        out_specs.append(wo_spec)
        out_shape.append(jax.ShapeDtypeStruct(w_out.shape, jnp.bfloat16))
        operands.append(w_out)
    outs = pl.pallas_call(
        kern,
        grid=(n_j, n_i),
        in_specs=in_specs,
        out_specs=out_specs,
        out_shape=out_shape,
        scratch_shapes=[pltpu.VMEM(wb_shape, jnp.bfloat16)],
        compiler_params=_params("parallel", "arbitrary"),
        name="in_proj",
    )(*operands)
    return outs if w_out is not None else outs[0]


def _split3(a):
    p1 = a.astype(jnp.bfloat16)
    r1 = a - p1.astype(jnp.float32)
    p2 = r1.astype(jnp.bfloat16)
    p3 = (r1 - p2.astype(jnp.float32)).astype(jnp.bfloat16)
    return p1, p2, p3


def _fox_q_kernel(x_ref, wq_ref, wt_ref, b_ref, wo_ref, xb_ref, q_ref, c_ref, wob_ref,
                  wqb_ref, carry_ref, *, scale):
    t = pl.program_id(1)
    ts = x_ref.shape[0]
    _round_slab(wo_ref, wob_ref)

    @pl.when((pl.program_id(0) == 0) & (t == 0))
    def _():
        wqb_ref[...] = wq_ref[...].astype(wqb_ref.dtype)

    @pl.when(t == 0)
    def _():
        carry_ref[...] = jnp.zeros_like(carry_ref)

    xb = x_ref[...].astype(jnp.bfloat16)
    xb_ref[...] = xb
    q_ref[...] = (_nt_dot(xb, wqb_ref[...]) * scale).astype(q_ref.dtype)
    f = _nt_dot(wt_ref[...].astype(jnp.bfloat16), xb)
    z = f + b_ref[...]
    logf = jnp.minimum(z, 0.0) - jnp.log1p(jnp.exp(-jnp.abs(z)))
    row = lax.broadcasted_iota(jnp.int32, (ts, ts), 0)
    col = lax.broadcasted_iota(jnp.int32, (ts, ts), 1)
    tri = (row <= col).astype(jnp.bfloat16)
    h = logf.shape[0]
    sums = jnp.dot(jnp.concatenate(_split3(logf), axis=0), tri,
                   preferred_element_type=jnp.float32)
    c = (sums[:h] + sums[h:2 * h]) + sums[2 * h:] + carry_ref[...]
    pieces = [p.astype(jnp.float32) for p in _split3(c * (-LOG2E))]
    pad = jnp.zeros((LANES - 3 * h, ts), jnp.float32)
    c_ref[...] = jnp.concatenate(pieces + [pad], axis=0).T.astype(c_ref.dtype)
    carry_ref[...] = c[:, ts - 1:ts]


def _fox_q_pass(x, w_in_t, n_q, b_f, scale, w_out):
    bsz, s, d = x.shape
    h = b_f.shape[0]
    ts = FOX_Q_TS
    nt = s // ts
    gate_block = (w_in_t.shape[0] - h) // h
    once = dict(pipeline_mode=pl.Buffered(1))
    wo_spec = _slab_spec(w_out, bsz * nt, lambda b, t: b * nt + t)
    return pl.pallas_call(
        functools.partial(_fox_q_kernel, scale=scale),
        grid=(bsz, nt),
        in_specs=[pl.BlockSpec((None, ts, d), lambda b, t: (b, t, 0)),
                  pl.BlockSpec((n_q, d), lambda b, t: (0, 0), **once),
                  pl.BlockSpec((h, d), lambda b, t: (gate_block, 0), **once),
                  pl.BlockSpec((h, 1), lambda b, t: (0, 0), **once),
                  wo_spec],
        out_specs=[pl.BlockSpec((None, ts, d), lambda b, t: (b, t, 0)),
                   pl.BlockSpec((None, ts, n_q), lambda b, t: (b, t, 0)),
                   pl.BlockSpec((None, ts, LANES), lambda b, t: (b, t, 0)),
                   wo_spec],
        out_shape=[jax.ShapeDtypeStruct((bsz, s, d), jnp.bfloat16),
                   jax.ShapeDtypeStruct((bsz, s, n_q), jnp.bfloat16),
                   jax.ShapeDtypeStruct((bsz, s, LANES), jnp.bfloat16),
                   jax.ShapeDtypeStruct(w_out.shape, jnp.bfloat16)],
        scratch_shapes=[pltpu.VMEM((n_q, d), jnp.bfloat16), pltpu.VMEM((h, 1), jnp.float32)],
        compiler_params=_params("arbitrary", "arbitrary"),
        name="fox_q_pass",
    )(x, w_in_t, w_in_t, b_f.reshape(h, 1), w_out)


def _causal_tile_mask(t):
    row = lax.broadcasted_iota(jnp.int32, (t, t), 0)
    col = lax.broadcasted_iota(jnp.int32, (t, t), 1)
    return col <= row


def _row_reduce(parts, combine, lane_reduce):
    tiles = [s[:, j:j + LANES] for s in parts for j in range(0, s.shape[1], LANES)]
    return lane_reduce(functools.reduce(combine, tiles), axis=-1, keepdims=True)


def _exp2_softmax_rows(parts, dtype, with_sum):
    m = _row_reduce(parts, jnp.maximum, jnp.max)
    ps = [jnp.exp2(s - m) for s in parts]
    p = jnp.concatenate([x.astype(dtype) for x in ps], axis=1)
    return (p, _row_reduce(ps, jnp.add, jnp.sum)) if with_sum else p


def _causal_softmax_rows(parts, dtype, with_sum):
    half = parts[-1].shape[0] // 2
    top = [x[:half] for x in parts[:-1]] + [parts[-1][:half, :half]]
    bot = [x[half:] for x in parts]
    rt = _exp2_softmax_rows(top, dtype, with_sum)
    rb = _exp2_softmax_rows(bot, dtype, with_sum)
    pt, pb = (rt[0], rb[0]) if with_sum else (rt, rb)
    p = jnp.concatenate([jnp.concatenate([pt, jnp.zeros((half, half), dtype)], axis=1), pb],
                        axis=0)
    return (p, jnp.concatenate([rt[1], rb[1]], axis=0)) if with_sum else p


def _pyramid_order(heads, n):
    items = [(h, i) for i in range(n) for h in range(heads)]
    return items[0::2] + items[1::2][::-1]


def _emit_pipelined(items, scores, softmax, output):
    n = len(items)
    s = {0: scores(items[0])}
    if n > 1:
        s[1] = scores(items[1])
    a = {0: softmax(items[0], s.pop(0))}
    for t in range(n):
        if t + 2 < n:
            s[t + 2] = scores(items[t + 2])
        if t + 1 < n:
            a[t + 1] = softmax(items[t + 1], s.pop(t + 1))
        output(items[t], a.pop(t))


def _fox_attn_kernel(q_ref, k_ref, v_ref, z_ref, c_ref, o_ref, va_ref, *, tq):
    n = q_ref.shape[0] // tq
    d = HEAD_DIM
    heads = q_ref.shape[1] // d
    half = tq // 2
    mask = _causal_tile_mask(half)
    lane = lax.broadcasted_iota(jnp.int32, (tq, LANES), 1)
    head0 = pl.program_id(1) * heads
    select = [((lane < 3 * FOX_HEADS) & (lane % FOX_HEADS == head0 + h)).astype(q_ref.dtype)
              for h in range(heads)]
    for h in range(heads):
        va_ref[h, :, :d] = v_ref[:, h * d:(h + 1) * d]
        va_ref[h, :, d:] = jnp.ones((va_ref.shape[1], d), va_ref.dtype)

    def scores(item):
        h, i = item
        lo = i * tq
        cols = slice(h * d, (h + 1) * d)
        qa = jnp.concatenate([q_ref[lo:lo + tq, cols], select[h]], axis=1)
        ka = jnp.concatenate([k_ref[:lo + tq, cols], c_ref[:lo + tq, :]], axis=1)
        s = _nt_dot(qa, ka)
        d0 = s[:, lo:]
        top = jnp.concatenate([jnp.where(mask, d0[:half, :half], MASK_VALUE), d0[:half, half:]],
                              axis=1)
        bot = jnp.concatenate([d0[half:, :half], jnp.where(mask, d0[half:, half:], MASK_VALUE)],
                              axis=1)
        diag = jnp.concatenate([top, bot], axis=0)
        return [s[:, :lo], diag] if i > 0 else [diag]

    def softmax(item, parts):
        return _causal_softmax_rows(parts, va_ref.dtype, with_sum=False)

    def output(item, p):
        h, i = item
        hi = (i + 1) * tq
        pv = jnp.dot(p, va_ref[h, :hi, :], preferred_element_type=jnp.float32)
        cols = slice(h * d, (h + 1) * d)
        gate = _silu(z_ref[hi - tq:hi, cols].astype(jnp.float32))
        o_ref[hi - tq:hi, cols] = (pv[:, :d] / pv[:, d:] * gate).astype(o_ref.dtype)

    _emit_pipelined(_pyramid_order(heads, n), scores, softmax, output)


def _fox_attention(q, kvz, c):
    bsz, s, _ = q.shape
    h, d, hs = FOX_HEADS, HEAD_DIM, FOX_HEADS_PER_STEP
    steps = h // hs
    w = hs * d
    kern = functools.partial(_fox_attn_kernel, tq=ATTN_TQ)
    return pl.pallas_call(
        kern,
        grid=(bsz, steps),
        in_specs=[pl.BlockSpec((None, s, w), lambda b, g: (b, 0, g)),
                  pl.BlockSpec((None, s, w), lambda b, g: (b, 0, g)),
                  pl.BlockSpec((None, s, w), lambda b, g: (b, 0, steps + g)),
                  pl.BlockSpec((None, s, w), lambda b, g: (b, 0, 2 * steps + g)),
                  pl.BlockSpec((None, s, LANES), lambda b, g: (b, 0, 0))],
        out_specs=pl.BlockSpec((None, s, w), lambda b, g: (b, 0, g)),
        out_shape=jax.ShapeDtypeStruct((bsz, s, h * d), jnp.bfloat16),
        scratch_shapes=[pltpu.VMEM((hs, s, 2 * d), jnp.bfloat16)],
        compiler_params=_params("parallel", "parallel"),
        name="fox_attn",
    )(q, kvz, kvz, kvz, c)


def _bias_kernel(tab_ref, o_ref):
    h = pl.program_id(0)
    t = o_ref.shape[1]
    u = lax.broadcasted_iota(jnp.int32, (8, 2 * t), 1)
    n = jnp.where(u <= t, t - u, 3 * t - u)
    max_exact = NUM_BUCKETS // 2
    large = max_exact + (jnp.log(jnp.maximum(n, 1).astype(jnp.float32) / max_exact)
                         / math.log(MAX_DISTANCE / max_exact)
                         * (NUM_BUCKETS - max_exact)).astype(jnp.int32)
    large = jnp.minimum(large, NUM_BUCKETS - 1)
    bucket = jnp.where(n < max_exact, n, large)
    val = jnp.zeros((8, 2 * t), jnp.float32)
    for b in range(NUM_BUCKETS):
        val = jnp.where(bucket == b, tab_ref[b, h], val)
    val = (val - tab_ref[NUM_BUCKETS - 1, h]) * LOG2E
    rows = jnp.concatenate([val] * (t // 8), axis=0)
    spread = pltpu.roll(rows, 0, 1, stride=1, stride_axis=0)
    o_ref[1] = spread[:, :t]
    o_ref[0] = jnp.where(_causal_tile_mask(t), spread[:, t:], MASK_VALUE)


def _bias_tiles(rel_bias, t):
    nh = rel_bias.shape[1]
    return pl.pallas_call(
        _bias_kernel,
        grid=(nh,),
        in_specs=[pl.BlockSpec(memory_space=pltpu.SMEM)],
        out_specs=pl.BlockSpec((None, 2, t, t), lambda h: (h, 0, 0, 0)),
        out_shape=jax.ShapeDtypeStruct((nh, 2, t, t), jnp.float32),
        compiler_params=_params("parallel"),
        name="bias_tiles",
    )(rel_bias)


def _diff_attn_kernel(q_ref, k_ref, v_ref, bias_ref, lq_ref, lk_ref, g_ref, o_ref,
                      *, tq, lambda_init):
    n = q_ref.shape[0] // tq
    d = HEAD_DIM
    heads = q_ref.shape[1] // (2 * d)
    lq = lq_ref[...]
    lk = lk_ref[...]
    dots = jnp.sum(lq * lk, axis=-1, keepdims=True)
    lam = jnp.exp(dots[0:1]) - jnp.exp(dots[1:2]) + lambda_init

    def scores(item):
        h, i = item
        lo = i * tq
        half = tq // 2
        maps = []
        for idx in range(2):
            cols = slice((2 * h + idx) * d, (2 * h + idx + 1) * d)
            s = _nt_dot(q_ref[lo:lo + tq, cols], k_ref[:lo + tq, cols])
            parts = [s[:, :lo - tq]] if i > 1 else []
            if i > 0:
                near = s[:, lo - tq:lo]
                corner = near[:half, half:] + bias_ref[h, 1, :half, half:]
                top = jnp.concatenate([near[:half, :half], corner], axis=1)
                parts.append(jnp.concatenate([top, near[half:]], axis=0))
            d0 = s[:, lo:]
            top = jnp.concatenate([d0[:half, :half] + bias_ref[h, 0, :half, :half],
                                   d0[:half, half:]], axis=1)
            parts.append(jnp.concatenate([top, d0[half:] + bias_ref[h, 0, half:, :]], axis=0))
            maps.append(parts)
        return maps

    def softmax(item, maps):
        p1, l1 = _causal_softmax_rows(maps[0], v_ref.dtype, with_sum=True)
        p2, l2 = _causal_softmax_rows(maps[1], v_ref.dtype, with_sum=True)
        return jnp.concatenate([p1, p2], axis=0), 1.0 / l1, lam / l2

    def output(item, args):
        h, i = item
        p, r1, r2 = args
        hi = (i + 1) * tq
        cols = slice(2 * h * d, 2 * (h + 1) * d)
        pv = jnp.dot(p, v_ref[:hi, cols], preferred_element_type=jnp.float32)
        o = pv[:tq] * r1 - pv[tq:] * r2
        o = o * lax.rsqrt(jnp.mean(o * o, axis=-1, keepdims=True) + RMS_EPS) * g_ref[...]
        o_ref[hi - tq:hi, cols] = (o * (1.0 - lambda_init)).astype(o_ref.dtype)

    _emit_pipelined([(h, i) for h in range(heads) for i in range(n)], scores, softmax, output)


def _diff_attention(proj, bias, lam_q, lam_k, subln_g, lambda_init):
    bsz, s, _ = proj.shape
    t = ATTN_TQ
    h, d2, hs = DIFF_HEADS, 2 * HEAD_DIM, DIFF_HEADS_PER_STEP
    steps = h // hs
    w = hs * d2
    kern = functools.partial(_diff_attn_kernel, tq=t, lambda_init=lambda_init)
    return pl.pallas_call(
        kern,
        grid=(bsz, steps),
        in_specs=[pl.BlockSpec((None, s, w), lambda b, g: (b, 0, g)),
                  pl.BlockSpec((None, s, w), lambda b, g: (b, 0, steps + g)),
                  pl.BlockSpec((None, s, w), lambda b, g: (b, 0, 2 * steps + g)),
                  pl.BlockSpec((hs, 2, t, t), lambda b, g: (g, 0, 0, 0)),
                  pl.BlockSpec((2, HEAD_DIM), lambda b, g: (0, 0)),
                  pl.BlockSpec((2, HEAD_DIM), lambda b, g: (0, 0)),
                  pl.BlockSpec((1, d2), lambda b, g: (0, 0))],
        out_specs=pl.BlockSpec((None, s, w), lambda b, g: (b, 0, g)),
        out_shape=jax.ShapeDtypeStruct((bsz, s, h * d2), jnp.bfloat16),
        compiler_params=_params("parallel", "parallel"),
        name="diff_attn",
    )(proj, proj, proj, bias, lam_q, lam_k, subln_g.reshape(1, d2))


def _out_ln_kernel(*refs, alpha, gate_here):
    if gate_here:
        o_ref, z_ref, x_ref, w_ref, g_ref, b_ref, y_ref, *maybe_yb_ref = refs
        gated = o_ref[...].astype(jnp.float32) * _silu(z_ref[...].astype(jnp.float32))
        gated = gated.astype(w_ref.dtype)
    else:
        o_ref, x_ref, w_ref, g_ref, b_ref, y_ref, *maybe_yb_ref = refs
        gated = o_ref[...]
    d = w_ref.shape[1]
    cw = d // OUT_N_CHUNKS
    chunks = [slice(c, c + cw) for c in range(0, d, cw)]
    vs = [alpha * x_ref[:, c] + jnp.dot(gated, w_ref[:, c], preferred_element_type=jnp.float32)
          for c in chunks]
    mu = sum(jnp.sum(v, axis=-1, keepdims=True) for v in vs) / d
    vcs = [v - mu for v in vs]
    var = sum(jnp.sum(vc * vc, axis=-1, keepdims=True) for vc in vcs) / d
    rs = lax.rsqrt(var + LN_EPS)
    for c, vc in zip(chunks, vcs):
        y = vc * rs * g_ref[:, c] + b_ref[:, c]
        y_ref[:, c] = y
        for yb_ref in maybe_yb_ref:
            yb_ref[:, c] = y.astype(yb_ref.dtype)


def _out_ln(o2d, z_src, x2d, w_bf16, g, b, alpha, emit_bf16):
    m, d = x2d.shape
    br = o2d.shape[1]
    tm = OUT_TM
    kern = functools.partial(_out_ln_kernel, alpha=alpha, gate_here=z_src is not None)
    resident = dict(pipeline_mode=pl.Buffered(1))
    in_specs = [pl.BlockSpec((tm, br), lambda i: (i, 0))]
    operands = [o2d]
    if z_src is not None:
        z_block = z_src.shape[1] // br - 1
        in_specs.append(pl.BlockSpec((tm, br), lambda i: (i, z_block)))
        operands.append(z_src)
    in_specs += [pl.BlockSpec((tm, d), lambda i: (i, 0)),
                 pl.BlockSpec((br, d), lambda i: (0, 0), **resident),
                 pl.BlockSpec((1, d), lambda i: (0, 0), **resident),
                 pl.BlockSpec((1, d), lambda i: (0, 0), **resident)]
    operands += [x2d, w_bf16, g.reshape(1, d), b.reshape(1, d)]
    out_specs = [pl.BlockSpec((tm, d), lambda i: (i, 0))]
    out_shape = [jax.ShapeDtypeStruct((m, d), jnp.float32)]
    if emit_bf16:
        out_specs.append(pl.BlockSpec((tm, d), lambda i: (i, 0)))
        out_shape.append(jax.ShapeDtypeStruct((m, d), jnp.bfloat16))
    return pl.pallas_call(
        kern,
        grid=(m // tm,),
        in_specs=in_specs,
        out_specs=out_specs,
        out_shape=out_shape,
        compiler_params=_params("parallel"),
        name="out_ln",
    )(*operands)


def kernel(x, fox_w_in, fox_b_f, fox_w_out, diff_w_in, diff_lam_q, diff_lam_k,
           diff_subln_g, diff_w_out, rel_bias, ln_g, ln_b):
    bsz, s, d = x.shape
    depth = ln_g.shape[0]
    branch = fox_w_out.shape[1]
    alpha = (2 * depth) ** 0.25
    scale = HEAD_DIM ** -0.5 * LOG2E
    m = bsz * s
    assert x.dtype == jnp.float32 and fox_w_out.shape[2] == d and diff_w_out.shape[2] == d
    assert branch == FOX_HEADS * HEAD_DIM == DIFF_HEADS * 2 * HEAD_DIM == diff_w_out.shape[1]
    assert fox_w_in.shape[1:] == (d, 4 * branch + FOX_HEADS)
    assert diff_w_in.shape[1:] == (d, 4 * branch) and rel_bias.shape == (NUM_BUCKETS, DIFF_HEADS)
    assert s % FOX_Q_TS == 0 and s % ATTN_TQ == 0 and ATTN_TQ >= 2 * MAX_DISTANCE
    assert m % PROJ_TM == 0 and m % OUT_TM == 0 and branch % PROJ_TN == 0
    assert FOX_HEADS % FOX_HEADS_PER_STEP == 0 and DIFF_HEADS % DIFF_HEADS_PER_STEP == 0
    assert d % (OUT_N_CHUNKS * LANES) == 0 and 3 * FOX_HEADS <= LANES

    x2d = x.reshape(m, d)
    xb2d = None
    for i in range(depth):
        j = i // 2
        if i % 2 == 0:
            w_in_t = fox_w_in[j].T
            xb, q, c, w_out_b = _fox_q_pass(x2d.reshape(bsz, s, d), w_in_t, branch, fox_b_f[j],
                                            scale, fox_w_out[j])
            proj = _project(xb.reshape(m, d), w_in_t, branch, 3 * branch, w_transposed=True)
            o = _fox_attention(q, proj.reshape(bsz, s, 3 * branch), c)
        else:
            lambda_init = 0.8 - 0.6 * math.exp(-0.3 * i)
            proj, w_out_b = _project(xb2d, diff_w_in[j], 0, 4 * branch, w_transposed=False,
                                     q_cols=branch, scale=scale, w_out=diff_w_out[j])
            bias = _bias_tiles(rel_bias, ATTN_TQ)
            o = _diff_attention(proj.reshape(bsz, s, 4 * branch), bias, diff_lam_q[j],
                                diff_lam_k[j], diff_subln_g[j], lambda_init)
        next_is_diff = i + 1 < depth and (i + 1) % 2 == 1
        outs = _out_ln(o.reshape(m, branch), None if i % 2 == 0 else proj, x2d, w_out_b,
                       ln_g[i], ln_b[i], alpha, emit_bf16=next_is_diff)
        x2d = outs[0]
        xb2d = outs[1] if next_is_diff else None
    return x2d.reshape(bsz, s, d)
```
